```python
import math
import jax
import jax.numpy as jnp
from jax import lax
import numpy as np

D_MODEL = 1024
BATCH = 32
SEQ = 2048
DEPTH = 4

RMS_EPS = 1e-6
D_FF = 2816

DN_HEADS = 4
DN_HEAD_DIM = 128
DN_WIDTH = DN_HEADS * DN_HEAD_DIM
DN_CONV = 5
DN_CHUNK = 64
L2_EPS = 1e-6

POOL_WINDOWS = (2, 4, 8, 16)
POOL_GROUPS = len(POOL_WINDOWS)
POOL_GROUP_DIM = 128
POOL_WIDTH = POOL_GROUPS * POOL_GROUP_DIM

DA_CONFIGS = ((128, 1), (512, 4), (2048, 16))
DA_NGROUPS = len(DA_CONFIGS)
DA_HEADS_PER_GROUP = 4
DA_HEAD_DIM = 64
DA_WIDTH = DA_NGROUPS * DA_HEADS_PER_GROUP * DA_HEAD_DIM
DA_OUT = DA_HEADS_PER_GROUP * DA_HEAD_DIM
DA_BLOCK = 64
ROPE_THETA = 10000.0
MASK_VALUE = -1e30

N_BRANCHES = 3

OFF_DN_QKV = 0
OFF_DN_Z = OFF_DN_QKV + 3 * DN_WIDTH
OFF_DN_BETA = OFF_DN_Z + DN_WIDTH
OFF_DN_A = OFF_DN_BETA + 2 * DN_HEADS
OFF_POOL = OFF_DN_A + 2 * DN_HEADS
OFF_DA = OFF_POOL + POOL_WIDTH
N_IN = OFF_DA + 3 * DA_WIDTH

kernel_name = 'hybrid_bidir_deltanet_pool_dilated_encoder'


def _rmsnorm(x, gain):
    xf = x.astype(jnp.float32)
    y = xf * lax.rsqrt(jnp.mean(xf * xf, axis=-1, keepdims=True) + RMS_EPS)
    return (y * gain.astype(jnp.float32)).astype(x.dtype)


def _l2norm(x):
    return x * lax.rsqrt(jnp.sum(x * x, axis=-1, keepdims=True) + L2_EPS)


def _swiglu(h, w_gate, w_up, w_down):
    return (jax.nn.silu(h @ w_gate) * (h @ w_up)) @ w_down


def _depthwise_conv_centred(x, w):
    K, C = w.shape
    return lax.conv_general_dilated(
        x, w[:, None, :], window_strides=(1,), padding=[(K // 2, K // 2)],
        dimension_numbers=('NWC', 'WIO', 'NWC'), feature_group_count=C)


def _gated_delta_chunked(q, k, v, g, beta):
    f32 = jnp.float32
    B_, H, S, Dk = q.shape
    Dv = v.shape[-1]
    C = DN_CHUNK
    N = S // C
    q = q.reshape(B_, H, N, C, Dk)
    k = k.reshape(B_, H, N, C, Dk)
    v = v.reshape(B_, H, N, C, Dv)
    beta = beta.reshape(B_, H, N, C)
    G = jnp.cumsum(g.reshape(B_, H, N, C), axis=-1)
    idx = jnp.arange(C)
    lower_incl = idx[:, None] >= idx[None, :]
    strict = idx[:, None] > idx[None, :]
    decay = jnp.exp(jnp.where(lower_incl, G[..., :, None] - G[..., None, :], -jnp.inf))
    kb = k * beta[..., None]
    kk = jnp.einsum('bhnid,bhnjd->bhnij', kb, k) * decay
    tri = jnp.where(strict, kk, 0.0) + jnp.eye(C, dtype=f32)
    rhs = jnp.concatenate([v * beta[..., None], kb * jnp.exp(G)[..., None]], axis=-1)
    sol = lax.linalg.triangular_solve(tri, rhs, left_side=True, lower=True, unit_diagonal=True)
    u, w = sol[..., :Dv], sol[..., Dv:]
    qk = jnp.where(lower_incl, jnp.einsum('bhnid,bhnjd->bhnij', q, k) * decay, 0.0)
    q_dec = q * jnp.exp(G)[..., None]
    k_dec = k * jnp.exp(G[..., -1:] - G)[..., None]
    g_last = jnp.exp(G[..., -1])

    def step(state, xs):
        qk_c, qd_c, kd_c, u_c, w_c, gl_c = xs
        v_new = u_c - jnp.einsum('bhck,bhkv->bhcv', w_c, state)
        o_c = (jnp.einsum('bhck,bhkv->bhcv', qd_c, state)
               + jnp.einsum('bhij,bhjv->bhiv', qk_c, v_new))
        state = state * gl_c[..., None, None] + jnp.einsum('bhck,bhcv->bhkv', kd_c, v_new)
        return state, o_c

    xs = tuple(jnp.moveaxis(t, 2, 0) for t in (qk, q_dec, k_dec, u, w, g_last))
    state0 = jnp.zeros((B_, H, Dk, Dv), f32)
    _, o = lax.scan(step, state0, xs)
    return jnp.moveaxis(o, 0, 2).reshape(B_, H, S, Dv)


def _deltanet_branch(qkv, z, beta_raw, a_raw, conv_w, a_log, dt_bias, out_norm):
    f32 = jnp.float32
    B_, S, _ = qkv.shape
    qkv = jax.nn.silu(_depthwise_conv_centred(qkv, conv_w)).astype(f32)

    def heads(t):
        return t.reshape(B_, S, DN_HEADS, DN_HEAD_DIM).transpose(0, 2, 1, 3)

    q = _l2norm(heads(qkv[..., :DN_WIDTH])) * (DN_HEAD_DIM ** -0.5)
    k = _l2norm(heads(qkv[..., DN_WIDTH:2 * DN_WIDTH]))
    v = heads(qkv[..., 2 * DN_WIDTH:])
    beta = jax.nn.sigmoid(beta_raw.astype(f32)).reshape(B_, S, 2, DN_HEADS).transpose(2, 0, 3, 1)
    g = (-jnp.exp(a_log.astype(f32))
         * jax.nn.softplus(a_raw.astype(f32).reshape(B_, S, 2, DN_HEADS) + dt_bias.astype(f32)))
    g = g.transpose(2, 0, 3, 1)
    o_fwd = _gated_delta_chunked(q, k, v, g[0], beta[0])
    flip = lambda t: jnp.flip(t, axis=2)
    o_bwd = flip(_gated_delta_chunked(flip(q), flip(k), flip(v), flip(g[1]), flip(beta[1])))
    o = (o_fwd + o_bwd).transpose(0, 2, 1, 3)
    o = _rmsnorm(o, out_norm) * jax.nn.silu(z.astype(f32).reshape(B_, S, DN_HEADS, DN_HEAD_DIM))
    return o.reshape(B_, S, DN_WIDTH).astype(z.dtype)


def _pooling_branch(u, pool_w, pool_scale):
    f32 = jnp.float32
    B_, S, _ = u.shape
    ug = u.astype(f32).reshape(B_, S, POOL_GROUPS, POOL_GROUP_DIM)
    csum = jnp.concatenate([jnp.zeros_like(ug[:, :1]), jnp.cumsum(ug, axis=1)], axis=1)
    pos = jnp.arange(S)
    outs = []
    for gi, win in enumerate(POOL_WINDOWS):
        lo = jnp.clip(pos - win // 2, 0, S)
        hi = jnp.clip(pos + (win - win // 2), 0, S)
        cnt = (hi - lo).astype(f32)
        cg = csum[:, :, gi]
        mean = (jnp.take(cg, hi, axis=1) - jnp.take(cg, lo, axis=1)) / cnt[None, :, None]
        outs.append(mean - ug[:, :, gi])
    pooled = jnp.stack(outs, axis=2)
    mixed = jnp.einsum('bsgc,gcd->bsgd', pooled, pool_w.astype(f32))
    return (mixed.reshape(B_, S, POOL_WIDTH) * pool_scale.astype(f32)).astype(u.dtype)


def _rope(x, pos):
    half = x.shape[-1] // 2
    inv_freq = ROPE_THETA ** (-jnp.arange(half, dtype=jnp.float32) / half)
    ang = pos.astype(jnp.float32)[:, None] * inv_freq[None, :]
    cos = jnp.cos(ang)[:, None, None, :]
    sin = jnp.sin(ang)[:, None, None, :]
    x1, x2 = x[..., :half], x[..., half:]
    return jnp.concatenate([x1 * cos - x2 * sin, x2 * cos + x1 * sin], axis=-1)


def _dilated_window_attention(q, k, v, dilation, radius):
    B_, S, H, Dh = q.shape
    L = S // dilation
    Q = DA_BLOCK
    nn = -(-radius // Q)
    nb = -(-L // Q)
    Lp = nb * Q

    def strided(t):
        return t.reshape(B_, L, dilation, H, Dh).transpose(0, 2, 1, 3, 4)

    qs, ks, vs = strided(q), strided(k), strided(v)
    qb = jnp.pad(qs, ((0, 0), (0, 0), (0, Lp - L), (0, 0), (0, 0))).reshape(B_, dilation, nb, Q, H, Dh)
    padk = ((0, 0), (0, 0), (nn * Q, Lp - L + nn * Q), (0, 0), (0, 0))
    kp = jnp.pad(ks, padk).reshape(B_, dilation, nb + 2 * nn, Q, H, Dh)
    vp = jnp.pad(vs, padk).reshape(B_, dilation, nb + 2 * nn, Q, H, Dh)
    kb = jnp.concatenate([kp[:, :, j:j + nb] for j in range(2 * nn + 1)], axis=3)
    vb = jnp.concatenate([vp[:, :, j:j + nb] for j in range(2 * nn + 1)], axis=3)
    s = jnp.einsum('brnqhd,brnkhd->brnhqk', qb, kb)
    blk = jnp.arange(nb)
    qpos = blk[:, None] * Q + jnp.arange(Q)[None, :]
    kpos = blk[:, None] * Q + jnp.arange((2 * nn + 1) * Q)[None, :] - nn * Q
    delta = kpos[:, None, :] - qpos[:, :, None]
    valid = (jnp.abs(delta) <= radius) & (kpos[:, None, :] >= 0) & (kpos[:, None, :] < L)
    s = jnp.where(valid[:, None], s, MASK_VALUE)
    lse = jax.nn.logsumexp(s, axis=-1)
    p = jnp.exp(s - lse[..., None])
    o = jnp.einsum('brnhqk,brnkhd->brnqhd', p, vb)
    o = o.reshape(B_, dilation, Lp, H, Dh)[:, :, :L].transpose(0, 2, 1, 3, 4).reshape(B_, S, H, Dh)
    lse = lse.transpose(0, 1, 2, 4, 3).reshape(B_, dilation, Lp, H)[:, :, :L]
    lse = lse.transpose(0, 2, 1, 3).reshape(B_, S, H)
    return o, lse


def _dilated_branch(qkv):
    f32 = jnp.float32
    B_, S, _ = qkv.shape
    t = qkv.astype(f32).reshape(B_, S, 3, DA_NGROUPS, DA_HEADS_PER_GROUP, DA_HEAD_DIM)
    pos = jnp.arange(S)
    q = _rope(t[:, :, 0], pos) * (DA_HEAD_DIM ** -0.5)
    k = _rope(t[:, :, 1], pos)
    v = t[:, :, 2]
    outs, lses = [], []
    for gi, (window, dil) in enumerate(DA_CONFIGS):
        o_g, lse_g = _dilated_window_attention(q[:, :, gi], k[:, :, gi], v[:, :, gi], dil, window // (2 * dil))
        outs.append(o_g)
        lses.append(lse_g)
    wts = jax.nn.softmax(jnp.stack(lses, axis=0), axis=0)
    merged = jnp.einsum('gbsh,gbshd->bshd', wts, jnp.stack(outs, axis=0))
    return merged.reshape(B_, S, DA_OUT).astype(qkv.dtype)


def setup_inputs(seed: int = 0) -> dict:
    key = jax.random.key(seed)
    ks = jax.random.split(key, 26)
    f32 = jnp.float32
    Lr, D, F = DEPTH, D_MODEL, D_FF

    def nrm(k_, shape, scale):
        return jax.random.normal(k_, shape, f32) * scale

    def gain(k_, shape):
        return 1.0 + 0.02 * jax.random.normal(k_, shape, f32)

    a_init = jax.random.uniform(ks[8], (Lr, 2, DN_HEADS), f32, 1.0, 16.0)
    dt = jnp.exp(jax.random.uniform(ks[9], (Lr, 2, DN_HEADS), f32, math.log(1e-3), math.log(1e-1)))
    return {
        'x': jax.random.normal(ks[0], (BATCH, SEQ, D), f32),
        'ffn1_norm': gain(ks[1], (Lr, D)),
        'ffn1_w_gate': nrm(ks[2], (Lr, D, F), D ** -0.5),
        'ffn1_w_up': nrm(ks[3], (Lr, D, F), D ** -0.5),
        'ffn1_w_down': nrm(ks[4], (Lr, F, D), F ** -0.5),
        'mix_norm': gain(ks[5], (Lr, D)),
        'w_in': nrm(ks[6], (Lr, D, N_IN), D ** -0.5),
        'dn_conv': nrm(ks[7], (Lr, DN_CONV, 3 * DN_WIDTH), DN_CONV ** -0.5),
        'dn_a_log': jnp.log(a_init),
        'dn_dt_bias': dt + jnp.log(-jnp.expm1(-dt)),
        'dn_out_norm': gain(ks[10], (Lr, DN_HEAD_DIM)),
        'pool_w': nrm(ks[11], (Lr, POOL_GROUPS, POOL_GROUP_DIM, POOL_GROUP_DIM), POOL_GROUP_DIM ** -0.5),
        'pool_scale': gain(ks[12], (Lr, POOL_WIDTH)),
        'w_proj_a': nrm(ks[13], (Lr, DN_WIDTH, D), DN_WIDTH ** -0.5),
        'w_proj_b': nrm(ks[14], (Lr, POOL_WIDTH, D), POOL_WIDTH ** -0.5),
        'w_proj_c': nrm(ks[15], (Lr, DA_OUT, D), DA_OUT ** -0.5),
        'w_gate': nrm(ks[16], (Lr, D, N_BRANCHES * D), D ** -0.5),
        'b_gate': nrm(ks[17], (Lr, N_BRANCHES * D), 0.01),
        'w_out': nrm(ks[18], (Lr, D, D), D ** -0.5),
        'ffn2_norm': gain(ks[19], (Lr, D)),
        'ffn2_w_gate': nrm(ks[20], (Lr, D, F), D ** -0.5),
        'ffn2_w_up': nrm(ks[21], (Lr, D, F), D ** -0.5),
        'ffn2_w_down': nrm(ks[22], (Lr, F, D), F ** -0.5),
        'final_norm': gain(ks[23], (D,)),
    }


def reference(x, ffn1_norm, ffn1_w_gate, ffn1_w_up, ffn1_w_down, mix_norm, w_in, dn_conv, dn_a_log,
              dn_dt_bias, dn_out_norm, pool_w, pool_scale, w_proj_a, w_proj_b, w_proj_c, w_gate, b_gate,
              w_out, ffn2_norm, ffn2_w_gate, ffn2_w_up, ffn2_w_down, final_norm):
    B_, S, D = x.shape
    for l in range(DEPTH):
        x = x + 0.5 * _swiglu(_rmsnorm(x, ffn1_norm[l]), ffn1_w_gate[l], ffn1_w_up[l], ffn1_w_down[l])
        h = _rmsnorm(x, mix_norm[l])
        proj = h @ w_in[l]
        y_a = _deltanet_branch(proj[..., OFF_DN_QKV:OFF_DN_Z], proj[..., OFF_DN_Z:OFF_DN_BETA],
                               proj[..., OFF_DN_BETA:OFF_DN_A], proj[..., OFF_DN_A:OFF_POOL],
                               dn_conv[l], dn_a_log[l], dn_dt_bias[l], dn_out_norm[l]) @ w_proj_a[l]
        y_b = _pooling_branch(proj[..., OFF_POOL:OFF_DA], pool_w[l], pool_scale[l]) @ w_proj_b[l]
        y_c = _dilated_branch(proj[..., OFF_DA:N_IN]) @ w_proj_c[l]
        gates = jax.nn.sigmoid(h @ w_gate[l] + b_gate[l]).reshape(B_, S, N_BRANCHES, D)
        merged = gates[:, :, 0] * y_a + gates[:, :, 1] * y_b + gates[:, :, 2] * y_c
        x = x + merged @ w_out[l]
        x = x + 0.5 * _swiglu(_rmsnorm(x, ffn2_norm[l]), ffn2_w_gate[l], ffn2_w_up[l], ffn2_w_down[l])
    return _rmsnorm(x, final_norm)
```

```python
import functools

import jax
import jax.numpy as jnp
from jax import lax
from jax.experimental import pallas as pl
from jax.experimental.pallas import tpu as pltpu

F32 = jnp.float32
BF16 = jnp.bfloat16
HIGHEST = lax.Precision.HIGHEST

RMS_EPS = 1e-6
L2_EPS = 1e-6

DN_HEADS = 4
DN_HEAD_DIM = 128
DN_WIDTH = DN_HEADS * DN_HEAD_DIM
DN_CONV = 5
DN_CHUNK = 64
DN_INV_BLOCK = 16

POOL_WINDOWS = (2, 4, 8, 16)
POOL_GROUP_DIM = 128
POOL_WIDTH = len(POOL_WINDOWS) * POOL_GROUP_DIM

DA_CONFIGS = ((128, 1), (512, 4), (2048, 16))
DA_NGROUPS = len(DA_CONFIGS)
DA_HEADS_PER_GROUP = 4
DA_HEAD_DIM = 64
DA_GROUP_WIDTH = DA_HEADS_PER_GROUP * DA_HEAD_DIM
DA_WIDTH = DA_NGROUPS * DA_GROUP_WIDTH
DA_QBLOCK = 128
ROPE_THETA = 10000.0
MASK_VALUE = -1e30

N_BRANCHES = 3

V7X_LANES = 128
V7X_SUBLANES = 8
V7X_VMEM_LIMIT_BYTES = 56 * 1024 * 1024

FFN_TOKEN_TILE = 512
MIX_TOKEN_TILE = 512
ROW_CHUNK = 256


def _params(*semantics):
    return pltpu.CompilerParams(dimension_semantics=semantics,
                                vmem_limit_bytes=V7X_VMEM_LIMIT_BYTES)


def _resident(shape):
    nd = len(shape)
    return pl.BlockSpec(shape, lambda *_: (0,) * nd, pipeline_mode=pl.Buffered(1))


def _rms(x, gain):
    return x * lax.rsqrt(jnp.mean(x * x, axis=-1, keepdims=True) + RMS_EPS) * gain


def _sigmoid(x):
    return 1.0 / (1.0 + jnp.exp(-x))


def _dot(a, b):
    return jnp.dot(a, b, preferred_element_type=F32)


def _dot_nt(a, b):
    return lax.dot_general(a, b, (((1,), (1,)), ((), ())), preferred_element_type=F32)


def _dot_tn(a, b):
    return lax.dot_general(a, b, (((0,), (0,)), ((), ())), preferred_element_type=F32)


def _dot_hi(a, b):
    return jnp.dot(a, b, preferred_element_type=F32, precision=HIGHEST)


def _ffn_kernel(x_ref, gain_ref, wg_ref, wu_ref, wd_ref, fgain_ref, o_ref, h_ref, acc_ref, *, final_norm):
    j = pl.program_id(1)

    @pl.when(j == 0)
    def _():
        h_ref[...] = _rms(x_ref[...], gain_ref[...]).astype(BF16)
        acc_ref[...] = jnp.zeros_like(acc_ref)

    h = h_ref[...]
    g = _dot(h, wg_ref[...])
    u = _dot(h, wu_ref[...])
    a = (g * _sigmoid(g) * u).astype(BF16)
    acc_ref[...] += _dot(a, wd_ref[...])

    @pl.when(j == pl.num_programs(1) - 1)
    def _():
        y = x_ref[...] + 0.5 * acc_ref[...]
        if final_norm:
            y = _rms(y, fgain_ref[...])
        o_ref[...] = y


def _ffn(x, gain, wg, wu, wd, fgain, *, final_norm):
    T, D = x.shape
    F = wg.shape[1]
    tm = FFN_TOKEN_TILE
    fc = F // 2
    return pl.pallas_call(
        functools.partial(_ffn_kernel, final_norm=final_norm),
        grid=(T // tm, F // fc),
        in_specs=[
            pl.BlockSpec((tm, D), lambda i, j: (i, 0)),
            pl.BlockSpec((1, D), lambda i, j: (0, 0)),
            pl.BlockSpec((D, fc), lambda i, j: (0, j)),
            pl.BlockSpec((D, fc), lambda i, j: (0, j)),
            pl.BlockSpec((fc, D), lambda i, j: (j, 0)),
            pl.BlockSpec((1, D), lambda i, j: (0, 0)),
        ],
        out_specs=pl.BlockSpec((tm, D), lambda i, j: (i, 0)),
        out_shape=jax.ShapeDtypeStruct((T, D), F32),
        scratch_shapes=[pltpu.VMEM((tm, D), BF16), pltpu.VMEM((tm, D), F32)],
        compiler_params=_params("parallel", "arbitrary"),
        name="ffn",
    )(x, gain, wg, wu, wd, fgain)


def _mix_in_kernel(x_ref, gain_ref, wqkv_ref, wz_ref, wba_ref, wpool_ref, wda_ref, cos_ref, sin_ref,
                   qkv_ref, z_ref, ba_ref, u_ref, daq_ref, dak_ref, dav_ref):
    h = _rms(x_ref[...], gain_ref[...]).astype(BF16)
    qkv_ref[...] = _dot(h, wqkv_ref[...])
    z_ref[...] = _dot(h, wz_ref[...])
    ba_ref[...] = _dot(h, wba_ref[...])
    u_ref[...] = _dot(h, wpool_ref[...])
    da = _dot(h, wda_ref[...])
    cos = cos_ref[...]
    sin = sin_ref[...]
    for part, out_ref, scale in ((0, daq_ref, DA_HEAD_DIM ** -0.5), (1, dak_ref, 1.0)):
        for cb in range(DA_WIDTH // V7X_LANES):
            lo = part * DA_WIDTH + cb * V7X_LANES
            t = da[:, lo:lo + V7X_LANES]
            r = t * cos + pltpu.roll(t, V7X_LANES // 2, 1) * sin
            out_ref[:, cb * V7X_LANES:(cb + 1) * V7X_LANES] = r * scale
    dav_ref[...] = da[:, 2 * DA_WIDTH:]


def _mix_in(x, gain, wqkv, wz, wba, wpool, wda, cos, sin, seq_len):
    T, D = x.shape
    tm = MIX_TOKEN_TILE
    tiles_per_seq = seq_len // tm
    tok = lambda n: pl.BlockSpec((tm, n), lambda i: (i, 0))
    return pl.pallas_call(
        _mix_in_kernel,
        grid=(T // tm,),
        in_specs=[
            tok(D),
            _resident((1, D)),
            _resident(wqkv.shape), _resident(wz.shape), _resident(wba.shape),
            _resident(wpool.shape), _resident(wda.shape),
            pl.BlockSpec((tm, V7X_LANES), lambda i: (i % tiles_per_seq, 0)),
            pl.BlockSpec((tm, V7X_LANES), lambda i: (i % tiles_per_seq, 0)),
        ],
        out_specs=[tok(3 * DN_WIDTH), tok(DN_WIDTH), tok(V7X_LANES), tok(POOL_WIDTH),
                   tok(DA_WIDTH), tok(DA_WIDTH), tok(DA_WIDTH)],
        out_shape=[
            jax.ShapeDtypeStruct((T, 3 * DN_WIDTH), F32),
            jax.ShapeDtypeStruct((T, DN_WIDTH), F32),
            jax.ShapeDtypeStruct((T, V7X_LANES), F32),
            jax.ShapeDtypeStruct((T, POOL_WIDTH), F32),
            jax.ShapeDtypeStruct((T, DA_WIDTH), F32),
            jax.ShapeDtypeStruct((T, DA_WIDTH), F32),
            jax.ShapeDtypeStruct((T, DA_WIDTH), F32),
        ],
        compiler_params=_params("parallel"),
        name="mix_in",
    )(x, gain, wqkv, wz, wba, wpool, wda, cos, sin)


def _unit_tri_inverse(a, eye, same_blk, n):
    d = jnp.where(same_blk, a, 0.0)
    e = a - d
    t = eye - d
    pw = d
    steps = DN_INV_BLOCK.bit_length() - 1
    for _ in range(steps - 1):
        pw = _dot_hi(pw, pw)
        t = t + _dot_hi(t, pw)
    m = _dot_hi(t, e)
    r = eye - m
    pw = m
    nblk = n // DN_INV_BLOCK
    for _ in range((nblk - 1).bit_length() - 1):
        pw = _dot_hi(pw, pw)
        r = r + _dot_hi(r, pw)
    return _dot_hi(r, t)


def _deltanet_kernel(alog_ref, dtb_ref, q_ref, k_ref, v_ref, z_ref, ba_ref, cq_ref, ck_ref, cv_ref,
                     onorm_ref, o_ref,
                     pad_ref, qn_ref, kn_ref, vn_ref, beta_ref, g_ref, of_ref, ob_ref, *, S, C):
    head = pl.program_id(1)
    RC = ROW_CHUNK
    HALO = V7X_SUBLANES
    n_rc = S // RC
    half = DN_CONV // 2

    def conv_phase(x_ref, w_ref, dst_ref, l2, scale):
        pad_ref[0:HALO, :] = jnp.zeros((HALO, DN_HEAD_DIM), F32)
        pad_ref[HALO + S:2 * HALO + S, :] = jnp.zeros((HALO, DN_HEAD_DIM), F32)
        pad_ref[HALO:HALO + S, :] = x_ref[0]
        w = w_ref[...]

        def body(i, carry):
            r0 = pl.multiple_of(i * RC, RC)
            win = pad_ref[pl.ds(r0, RC + 2 * HALO), :]
            acc = jnp.zeros((RC, DN_HEAD_DIM), F32)
            for j in range(DN_CONV):
                off = HALO + j - half
                acc = acc + win[off:off + RC, :] * w[j:j + 1, :]
            y = acc * _sigmoid(acc)
            if l2:
                y = y * lax.rsqrt(jnp.sum(y * y, axis=-1, keepdims=True) + L2_EPS)
            dst_ref[pl.ds(r0, RC), :] = y * scale
            return carry

        lax.fori_loop(0, n_rc, body, 0)

    conv_phase(q_ref, cq_ref, qn_ref, True, DN_HEAD_DIM ** -0.5)
    conv_phase(k_ref, ck_ref, kn_ref, True, 1.0)
    conv_phase(v_ref, cv_ref, vn_ref, False, 1.0)

    lane = lax.broadcasted_iota(jnp.int32, (1, V7X_LANES), 1)

    def gate_body(i, carry):
        r0 = pl.multiple_of(i * RC, RC)
        ba = ba_ref[0, pl.ds(r0, RC), :]

        def col(c):
            return jnp.sum(jnp.where(lane == c, ba, 0.0), axis=-1, keepdims=True)

        for d in range(2):
            beta = _sigmoid(col(d * DN_HEADS + head))
            beta_ref[d, pl.ds(r0, RC), :] = jnp.broadcast_to(beta, (RC, DN_HEAD_DIM))
            a_raw = col(2 * DN_HEADS + d * DN_HEADS + head) + dtb_ref[d, head]
            softplus = jnp.maximum(a_raw, 0.0) + jnp.log(1.0 + jnp.exp(-jnp.abs(a_raw)))
            rate = jnp.exp(jnp.full((1, 1), alog_ref[d, head], F32))
            g_ref[d, pl.ds(r0, RC), :] = jnp.broadcast_to(-rate * softplus, (RC, DN_HEAD_DIM))
        return carry

    lax.fori_loop(0, n_rc, gate_body, 0)

    ri = lax.broadcasted_iota(jnp.int32, (C, C), 0)
    ci = lax.broadcasted_iota(jnp.int32, (C, C), 1)
    eye_b = ri == ci
    eye = eye_b.astype(F32)
    ones = jnp.ones((C, C), F32)
    same_blk = (ri // DN_INV_BLOCK) == (ci // DN_INV_BLOCK)
    n_chunks = S // C

    def one_direction(d, c_idx, state):
        incl = (ri >= ci) if d == 0 else (ri <= ci)
        strict = (ri > ci) if d == 0 else (ri < ci)
        last = C - 1 if d == 0 else 0
        rows = pl.ds(pl.multiple_of(c_idx * C, C), C)
        q = qn_ref[rows, :]
        k = kn_ref[rows, :]
        v = vn_ref[rows, :]
        beta = beta_ref[d, rows, :]
        g = g_ref[d, rows, :]
        gc = _dot_hi(incl.astype(F32), g)
        gr = _dot_hi(ones, jnp.where(eye_b, gc[:, :C], 0.0))
        decay = jnp.where(incl, jnp.exp(gc[:, :C] - gr), 0.0)
        eg = jnp.exp(gc)
        g_last = gc[last:last + 1, :]
        kb = k * beta
        kb16 = kb.astype(BF16)
        k16 = k.astype(BF16)
        a = jnp.where(strict, _dot_nt(kb16, k16) * decay, 0.0)
        t = _unit_tri_inverse(a, eye, same_blk, C)
        rhs = jnp.concatenate([v * beta, kb * eg], axis=1)
        sol = _dot_hi(t, rhs)
        u = sol[:, :DN_HEAD_DIM]
        w = sol[:, DN_HEAD_DIM:]
        qk = _dot_nt(q.astype(BF16), k16) * decay
        q_dec = (q * eg).astype(BF16)
        k_dec = (k * jnp.exp(g_last - gc)).astype(BF16)
        s16 = state.astype(BF16)
        v_new = u - _dot(w.astype(BF16), s16)
        vn16 = v_new.astype(BF16)
        o = _dot(q_dec, s16) + _dot(qk.astype(BF16), vn16)
        state = state * jnp.exp(g_last) + _dot_tn(k_dec, vn16)
        return o, state

    def chunk_body(c, carry):
        sf, sb = carry
        o_f, sf = one_direction(0, c, sf)
        of_ref[pl.ds(pl.multiple_of(c * C, C), C), :] = o_f
        cb = n_chunks - 1 - c
        o_b, sb = one_direction(1, cb, sb)
        ob_ref[pl.ds(pl.multiple_of(cb * C, C), C), :] = o_b
        return sf, sb

    zero_state = jnp.zeros((DN_HEAD_DIM, DN_HEAD_DIM), F32)
    lax.fori_loop(0, n_chunks, chunk_body, (zero_state, zero_state))

    def out_body(i, carry):
        r0 = pl.multiple_of(i * RC, RC)
        o = of_ref[pl.ds(r0, RC), :] + ob_ref[pl.ds(r0, RC), :]
        z = z_ref[0, pl.ds(r0, RC), :]
        y = _rms(o, onorm_ref[...]) * (z * _sigmoid(z))
        o_ref[0, pl.ds(r0, RC), :] = y.astype(o_ref.dtype)
        return carry

    lax.fori_loop(0, n_rc, out_body, 0)


def _deltanet(qkv, z, ba, conv_w, a_log, dt_bias, out_norm):
    B, S, _ = qkv.shape
    C = DN_CHUNK
    hd = DN_HEAD_DIM
    seq = lambda off: pl.BlockSpec((1, S, hd), lambda b, h: (b, 0, off + h))
    cw = lambda off: pl.BlockSpec((DN_CONV, hd), lambda b, h: (0, off + h))
    smem = pl.BlockSpec(memory_space=pltpu.SMEM)
    return pl.pallas_call(
        functools.partial(_deltanet_kernel, S=S, C=C),
        grid=(B, DN_HEADS),
        in_specs=[
            smem, smem,
            seq(0), seq(DN_HEADS), seq(2 * DN_HEADS),
            seq(0),
            pl.BlockSpec((1, S, V7X_LANES), lambda b, h: (b, 0, 0)),
            cw(0), cw(DN_HEADS), cw(2 * DN_HEADS),
            pl.BlockSpec((1, hd), lambda b, h: (0, 0)),
        ],
        out_specs=pl.BlockSpec((1, S, hd), lambda b, h: (b, 0, h)),
        out_shape=jax.ShapeDtypeStruct((B, S, DN_WIDTH), BF16),
        scratch_shapes=[
            pltpu.VMEM((S + 2 * V7X_SUBLANES, hd), F32),
            pltpu.VMEM((S, hd), F32), pltpu.VMEM((S, hd), F32), pltpu.VMEM((S, hd), F32),
            pltpu.VMEM((2, S, hd), F32), pltpu.VMEM((2, S, hd), F32),
            pltpu.VMEM((S, hd), F32), pltpu.VMEM((S, hd), F32),
        ],
        compiler_params=_params("parallel", "parallel"),
        name="deltanet",
    )(a_log, dt_bias, qkv, qkv, qkv, z, ba, conv_w, conv_w, conv_w, out_norm)


def _pool_kernel(u_ref, w_ref, sc_ref, o_ref, pad_ref, *, S):
    grp = pl.program_id(1)
    RC = ROW_CHUNK
    HALO = 2 * V7X_SUBLANES
    gd = POOL_GROUP_DIM
    pad_ref[0:HALO, :] = jnp.zeros((HALO, gd), F32)
    pad_ref[HALO + S:2 * HALO + S, :] = jnp.zeros((HALO, gd), F32)
    pad_ref[HALO:HALO + S, :] = u_ref[0]
    w16 = w_ref[0].astype(BF16)
    scale = sc_ref[...]

    for gi, win in enumerate(POOL_WINDOWS):
        lo_off = -(win // 2)
        hi_off = win - win // 2

        @pl.when(grp == gi)
        def _():
            def body(i, carry):
                r0 = pl.multiple_of(i * RC, RC)
                blk = pad_ref[pl.ds(r0, RC + 2 * HALO), :]
                acc = jnp.zeros((RC, gd), F32)
                for off in range(lo_off, hi_off):
                    acc = acc + blk[HALO + off:HALO + off + RC, :]
                pos = r0 + lax.broadcasted_iota(jnp.int32, (RC, 1), 0)
                cnt = jnp.minimum(pos + hi_off, S) - jnp.maximum(pos + lo_off, 0)
                pooled = acc / cnt.astype(F32) - blk[HALO:HALO + RC, :]
                mixed = _dot(pooled.astype(BF16), w16) * scale
                o_ref[0, pl.ds(r0, RC), :] = mixed.astype(o_ref.dtype)
                return carry

            lax.fori_loop(0, S // RC, body, 0)


def _pool(u, pool_w, pool_scale):
    B, S, _ = u.shape
    gd = POOL_GROUP_DIM
    return pl.pallas_call(
        functools.partial(_pool_kernel, S=S),
        grid=(B, len(POOL_WINDOWS)),
        in_specs=[
            pl.BlockSpec((1, S, gd), lambda b, g: (b, 0, g)),
            pl.BlockSpec((1, gd, gd), lambda b, g: (g, 0, 0)),
            pl.BlockSpec((1, gd), lambda b, g: (0, g)),
        ],
        out_specs=pl.BlockSpec((1, S, gd), lambda b, g: (b, 0, g)),
        out_shape=jax.ShapeDtypeStruct((B, S, POOL_WIDTH), BF16),
        scratch_shapes=[pltpu.VMEM((S + 4 * V7X_SUBLANES, gd), F32)],
        compiler_params=_params("parallel", "parallel"),
        name="pool",
    )(u, pool_w, pool_scale)


def _dilated_kernel(q_ref, k_ref, v_ref, o_ref,
                    qd_ref, kd_ref, vd_ref, od_ref, ld_ref, on_ref, ln_ref, num_ref, den_ref, mx_ref, *, S):
    grp = pl.program_id(2)
    QB = DA_QBLOCK
    W = V7X_LANES
    n_it = S // QB
    lane = lax.broadcasted_iota(jnp.int32, (1, W), 1)
    pair_mask = [((lane // (DA_HEAD_DIM // 2)) % 2 == e) for e in range(2)]
    first_head = lane < DA_HEAD_DIM
    qi = lax.broadcasted_iota(jnp.int32, (QB, 2 * QB), 0)
    kj = lax.broadcasted_iota(jnp.int32, (QB, 2 * QB), 1)

    for gi, (window, dil) in enumerate(DA_CONFIGS):
        radius = window // (2 * dil)
        assert radius == QB // 2
        L = S // dil
        LP = L + QB
        blocks_per_res = L // QB
        band = jnp.abs(kj - radius - qi) <= radius

        @pl.when(grp == gi)
        def _():
            zpad = jnp.zeros((radius, W), BF16)
            for r in range(dil):
                src = pl.ds(r, L, stride=dil) if dil > 1 else pl.ds(0, L)
                qd_ref[r * L:(r + 1) * L, :] = q_ref[0, src, :].astype(BF16)
                base = r * LP
                kd_ref[base:base + radius, :] = zpad
                vd_ref[base:base + radius, :] = zpad
                kd_ref[base + radius:base + radius + L, :] = k_ref[0, src, :].astype(BF16)
                vd_ref[base + radius:base + radius + L, :] = v_ref[0, src, :].astype(BF16)
                kd_ref[base + radius + L:base + LP, :] = zpad
                vd_ref[base + radius + L:base + LP, :] = zpad

            def body(it, carry):
                res = it // blocks_per_res
                m0 = (it % blocks_per_res) * QB
                q0 = pl.multiple_of(it * QB, QB)
                k0 = pl.multiple_of(it * QB + res * QB, QB)
                kpos = kj + (m0 - radius)
                valid = band & (kpos >= 0) & (kpos < L)
                qp = qd_ref[pl.ds(q0, QB), :]
                kp = kd_ref[pl.ds(k0, 2 * QB), :]
                vp = vd_ref[pl.ds(k0, 2 * QB), :]
                outs, lses = [], []
                for e in range(2):
                    qm = jnp.where(pair_mask[e], qp, jnp.zeros_like(qp))
                    s = jnp.where(valid, _dot_nt(qm, kp), MASK_VALUE)
                    mx = jnp.max(s, axis=-1, keepdims=True)
                    pe = jnp.exp(s - mx)
                    den = jnp.sum(pe, axis=-1, keepdims=True)
                    outs.append(_dot(pe.astype(BF16), vp) / den)
                    lses.append(mx + jnp.log(den))
                od_ref[pl.ds(q0, QB), :] = jnp.where(first_head, outs[0], outs[1])
                ld_ref[pl.ds(q0, QB), :] = jnp.where(first_head, lses[0], lses[1])
                return carry

            lax.fori_loop(0, n_it, body, 0)

            for r in range(dil):
                dst = pl.ds(r, L, stride=dil) if dil > 1 else pl.ds(0, L)
                on_ref[dst, :] = od_ref[r * L:(r + 1) * L, :]
                ln_ref[dst, :] = ld_ref[r * L:(r + 1) * L, :]

    @pl.when(grp == 0)
    def _():
        num_ref[...] = on_ref[...]
        den_ref[...] = jnp.ones_like(den_ref)
        mx_ref[...] = ln_ref[...]

    @pl.when(grp > 0)
    def _():
        m_old = mx_ref[...]
        lse = ln_ref[...]
        m_new = jnp.maximum(m_old, lse)
        a_old = jnp.exp(m_old - m_new)
        a_new = jnp.exp(lse - m_new)
        num_ref[...] = num_ref[...] * a_old + on_ref[...] * a_new
        den_ref[...] = den_ref[...] * a_old + a_new
        mx_ref[...] = m_new

    @pl.when(grp == DA_NGROUPS - 1)
    def _():
        o_ref[0] = (num_ref[...] / den_ref[...]).astype(o_ref.dtype)


def _dilated(daq, dak, dav):
    B, S, _ = daq.shape
    W = V7X_LANES
    pairs = DA_GROUP_WIDTH // W
    max_dil = max(d for _, d in DA_CONFIGS)
    grp_spec = pl.BlockSpec((1, S, W), lambda b, p, g: (b, 0, g * pairs + p))
    full = lambda dt: pltpu.VMEM((S, W), dt)
    return pl.pallas_call(
        functools.partial(_dilated_kernel, S=S),
        grid=(B, pairs, DA_NGROUPS),
        in_specs=[grp_spec, grp_spec, grp_spec],
        out_specs=pl.BlockSpec((1, S, W), lambda b, p, g: (b, 0, p)),
        out_shape=jax.ShapeDtypeStruct((B, S, DA_GROUP_WIDTH), BF16),
        scratch_shapes=[
            full(BF16),
            pltpu.VMEM((S + max_dil * DA_QBLOCK, W), BF16),
            pltpu.VMEM((S + max_dil * DA_QBLOCK, W), BF16),
            full(F32), full(F32), full(F32), full(F32), full(F32), full(F32), full(F32),
        ],
        compiler_params=_params("parallel", "parallel", "arbitrary"),
        name="dilated",
    )(daq, dak, dav)


def _mix_out_kernel(x_ref, gain_ref, ya_ref, yb_ref, yc_ref, wg_ref, bg_ref, wa_ref, wb_ref, wc_ref,
                    wo_ref, o_ref):
    x = x_ref[...]
    D = x.shape[-1]
    h = _rms(x, gain_ref[...]).astype(BF16)
    merged = jnp.zeros(x.shape, F32)
    for i, (y_ref, w_ref) in enumerate(((ya_ref, wa_ref), (yb_ref, wb_ref), (yc_ref, wc_ref))):
        gate = _sigmoid(_dot(h, wg_ref[:, i * D:(i + 1) * D]) + bg_ref[:, i * D:(i + 1) * D])
        merged = merged + gate * _dot(y_ref[...], w_ref[...])
    o_ref[...] = x + _dot(merged.astype(BF16), wo_ref[...])


def _mix_out(x, gain, ya, yb, yc, wg, bg, wa, wb, wc, wo):
    T, D = x.shape
    tm = MIX_TOKEN_TILE
    tok = lambda n: pl.BlockSpec((tm, n), lambda i: (i, 0))
    return pl.pallas_call(
        _mix_out_kernel,
        grid=(T // tm,),
        in_specs=[
            tok(D), _resident((1, D)),
            tok(ya.shape[1]), tok(yb.shape[1]), tok(yc.shape[1]),
            _resident(wg.shape), _resident(bg.shape), _resident(wa.shape), _resident(wb.shape),
            _resident(wc.shape), _resident(wo.shape),
        ],
        out_specs=tok(D),
        out_shape=jax.ShapeDtypeStruct((T, D), F32),
        compiler_params=_params("parallel"),
        name="mix_out",
    )(x, gain, ya, yb, yc, wg, bg, wa, wb, wc, wo)


def _pair_layout_columns():
    half = DA_HEAD_DIM // 2
    order = []
    for g in range(DA_NGROUPS):
        for p in range(DA_HEADS_PER_GROUP // 2):
            for part in range(2):
                for e in range(2):
                    head = 2 * p + e
                    start = g * DA_GROUP_WIDTH + head * DA_HEAD_DIM + part * half
                    order.extend(range(start, start + half))
    return jnp.asarray(order, dtype=jnp.int32)


def _rope_tables(seq_len):
    half = DA_HEAD_DIM // 2
    inv_freq = ROPE_THETA ** (-jnp.arange(half, dtype=F32) / half)
    ang = jnp.arange(seq_len).astype(F32)[:, None] * inv_freq[None, :]
    cos = jnp.tile(jnp.cos(ang), (1, V7X_LANES // half))
    sin = jnp.sin(ang)
    sin = jnp.concatenate([-sin, -sin, sin, sin], axis=1)
    return cos, sin


def kernel(x, ffn1_norm, ffn1_w_gate, ffn1_w_up, ffn1_w_down, mix_norm, w_in, dn_conv, dn_a_log, dn_dt_bias, dn_out_norm, pool_w, pool_scale, w_proj_a, w_proj_b, w_proj_c, w_gate, b_gate, w_out, ffn2_norm, ffn2_w_gate, ffn2_w_up, ffn2_w_down, final_norm):
    B, S, D = x.shape
    depth = w_in.shape[0]
    T = B * S
    off_z = 3 * DN_WIDTH
    off_beta = off_z + DN_WIDTH
    off_pool = off_beta + 4 * DN_HEADS
    off_da = off_pool + POOL_WIDTH
    pair_cols = _pair_layout_columns()
    cos, sin = _rope_tables(S)
    fgain = final_norm.reshape(1, D)

    xt = x.reshape(T, D)
    for l in range(depth):
        xt = _ffn(xt, ffn1_norm[l].reshape(1, D), ffn1_w_gate[l].astype(BF16), ffn1_w_up[l].astype(BF16),
                  ffn1_w_down[l].astype(BF16), fgain, final_norm=False)

        w = w_in[l]
        w_da = w[:, off_da:]
        w_da = jnp.concatenate([w_da[:, :DA_WIDTH][:, pair_cols],
                                w_da[:, DA_WIDTH:2 * DA_WIDTH][:, pair_cols],
                                w_da[:, 2 * DA_WIDTH:]], axis=1)
        w_ba = jnp.pad(w[:, off_beta:off_pool], ((0, 0), (0, V7X_LANES - 4 * DN_HEADS)))
        qkv, z, ba, u, daq, dak, dav = _mix_in(
            xt, mix_norm[l].reshape(1, D), w[:, :off_z].astype(BF16), w[:, off_z:off_beta].astype(BF16),
            w_ba.astype(BF16), w[:, off_pool:off_da].astype(BF16), w_da.astype(BF16), cos, sin, S)

        y_a = _deltanet(qkv.reshape(B, S, -1), z.reshape(B, S, -1), ba.reshape(B, S, -1), dn_conv[l],
                        dn_a_log[l], dn_dt_bias[l], dn_out_norm[l].reshape(1, DN_HEAD_DIM))
        y_b = _pool(u.reshape(B, S, -1), pool_w[l], pool_scale[l].reshape(1, POOL_WIDTH))
        y_c = _dilated(daq.reshape(B, S, -1), dak.reshape(B, S, -1), dav.reshape(B, S, -1))

        xt = _mix_out(xt, mix_norm[l].reshape(1, D), y_a.reshape(T, -1), y_b.reshape(T, -1),
                      y_c.reshape(T, -1), w_gate[l].astype(BF16), b_gate[l].reshape(1, -1),
                      w_proj_a[l].astype(BF16), w_proj_b[l].astype(BF16), w_proj_c[l].astype(BF16),
                      w_out[l].astype(BF16))

        xt = _ffn(xt, ffn2_norm[l].reshape(1, D), ffn2_w_gate[l].astype(BF16), ffn2_w_up[l].astype(BF16),
                  ffn2_w_down[l].astype(BF16), fgain, final_norm=(l == depth - 1))
    return xt.reshape(B, S, D)
```

```python
import functools

import jax
import jax.numpy as jnp
from jax import lax
from jax.experimental import pallas as pl
from jax.experimental.pallas import tpu as pltpu

F32 = jnp.float32
BF16 = jnp.bfloat16

RMS_EPS = 1e-6
L2_EPS = 1e-6

DN_HEADS = 4
DN_HEAD_DIM = 128
DN_WIDTH = DN_HEADS * DN_HEAD_DIM
DN_CONV = 5
DN_CHUNK = 128
DN_PREP_CHUNKS = 4

POOL_WINDOWS = (2, 4, 8, 16)
POOL_GROUP_DIM = 128
POOL_WIDTH = len(POOL_WINDOWS) * POOL_GROUP_DIM

DA_CONFIGS = ((128, 1), (512, 4), (2048, 16))
DA_NGROUPS = len(DA_CONFIGS)
DA_HEADS_PER_GROUP = 4
DA_HEAD_DIM = 64
DA_GROUP_WIDTH = DA_HEADS_PER_GROUP * DA_HEAD_DIM
DA_WIDTH = DA_NGROUPS * DA_GROUP_WIDTH
DA_QBLOCK = 128
ROPE_THETA = 10000.0
MASK_VALUE = -1e30

N_BRANCHES = 3

V7X_LANES = 128
V7X_SUBLANES = 8
V7X_VMEM_LIMIT_BYTES = 56 * 1024 * 1024

FFN_TOKEN_TILE = 512
MIX_TOKEN_TILE = 512
ROW_CHUNK = 256


def _params(*semantics):
    return pltpu.CompilerParams(dimension_semantics=semantics,
                                vmem_limit_bytes=V7X_VMEM_LIMIT_BYTES)


def _resident(shape):
    nd = len(shape)
    return pl.BlockSpec(shape, lambda *_: (0,) * nd, pipeline_mode=pl.Buffered(1))


def _rms(x, gain):
    return x * lax.rsqrt(jnp.mean(x * x, axis=-1, keepdims=True) + RMS_EPS) * gain


def _sigmoid(x):
    return 1.0 / (1.0 + jnp.exp(-x))


def _dot(a, b):
    return jnp.dot(a, b, preferred_element_type=F32)


def _dot_nt(a, b):
    return lax.dot_general(a, b, (((1,), (1,)), ((), ())), preferred_element_type=F32)


def _dot_tn(a, b):
    return lax.dot_general(a, b, (((0,), (0,)), ((), ())), preferred_element_type=F32)


def _split16(a):
    hi = a.astype(BF16)
    return hi, (a - hi.astype(F32)).astype(BF16)


def _mm(a, b):
    return _dot(a.astype(BF16), b.astype(BF16))


def _ffn_kernel(x_ref, gain_ref, wg_ref, wu_ref, wd_ref, fgain_ref, o_ref, h_ref, acc_ref, *, final_norm):
    j = pl.program_id(1)

    @pl.when(j == 0)
    def _():
        h_ref[...] = _rms(x_ref[...], gain_ref[...]).astype(BF16)
        acc_ref[...] = jnp.zeros_like(acc_ref)

    h = h_ref[...]
    g = _dot(h, wg_ref[...])
    u = _dot(h, wu_ref[...])
    a = (g * _sigmoid(g) * u).astype(BF16)
    acc_ref[...] += _dot(a, wd_ref[...])

    @pl.when(j == pl.num_programs(1) - 1)
    def _():
        y = x_ref[...] + 0.5 * acc_ref[...]
        if final_norm:
            y = _rms(y, fgain_ref[...])
        o_ref[...] = y


def _ffn(x, gain, wg, wu, wd, fgain, *, final_norm):
    T, D = x.shape
    F = wg.shape[1]
    tm = FFN_TOKEN_TILE
    fc = F // 2
    return pl.pallas_call(
        functools.partial(_ffn_kernel, final_norm=final_norm),
        grid=(T // tm, F // fc),
        in_specs=[
            pl.BlockSpec((tm, D), lambda i, j: (i, 0)),
            pl.BlockSpec((1, D), lambda i, j: (0, 0)),
            pl.BlockSpec((D, fc), lambda i, j: (0, j)),
            pl.BlockSpec((D, fc), lambda i, j: (0, j)),
            pl.BlockSpec((fc, D), lambda i, j: (j, 0)),
            pl.BlockSpec((1, D), lambda i, j: (0, 0)),
        ],
        out_specs=pl.BlockSpec((tm, D), lambda i, j: (i, 0)),
        out_shape=jax.ShapeDtypeStruct((T, D), F32),
        scratch_shapes=[pltpu.VMEM((tm, D), BF16), pltpu.VMEM((tm, D), F32)],
        compiler_params=_params("parallel", "arbitrary"),
        name="ffn",
    )(x, gain, wg, wu, wd, fgain)


def _mix_in_kernel(x_ref, gain_ref, wqkv_ref, wz_ref, wba_ref, wpool_ref, wda_ref, cos_ref, sin_ref,
                   qkv_ref, z_ref, ba_ref, u_ref, daq_ref, dak_ref, dav_ref):
    h = _rms(x_ref[...], gain_ref[...]).astype(BF16)
    qkv_ref[...] = _dot(h, wqkv_ref[...])
    z_ref[...] = _dot(h, wz_ref[...])
    ba_ref[...] = _dot(h, wba_ref[...])
    u_ref[...] = _dot(h, wpool_ref[...])
    da = _dot(h, wda_ref[...])
    cos = cos_ref[...]
    sin = sin_ref[...]
    for part, out_ref, scale in ((0, daq_ref, DA_HEAD_DIM ** -0.5), (1, dak_ref, 1.0)):
        for cb in range(DA_WIDTH // V7X_LANES):
            lo = part * DA_WIDTH + cb * V7X_LANES
            t = da[:, lo:lo + V7X_LANES]
            r = t * cos + pltpu.roll(t, V7X_LANES // 2, 1) * sin
            out_ref[:, cb * V7X_LANES:(cb + 1) * V7X_LANES] = r * scale
    dav_ref[...] = da[:, 2 * DA_WIDTH:]


def _mix_in(x, gain, wqkv, wz, wba, wpool, wda, cos, sin, seq_len):
    T, D = x.shape
    tm = MIX_TOKEN_TILE
    tiles_per_seq = seq_len // tm
    tok = lambda n: pl.BlockSpec((tm, n), lambda i: (i, 0))
    return pl.pallas_call(
        _mix_in_kernel,
        grid=(T // tm,),
        in_specs=[
            tok(D),
            _resident((1, D)),
            _resident(wqkv.shape), _resident(wz.shape), _resident(wba.shape),
            _resident(wpool.shape), _resident(wda.shape),
            pl.BlockSpec((tm, V7X_LANES), lambda i: (i % tiles_per_seq, 0)),
            pl.BlockSpec((tm, V7X_LANES), lambda i: (i % tiles_per_seq, 0)),
        ],
        out_specs=[tok(3 * DN_WIDTH), tok(DN_WIDTH), tok(V7X_LANES), tok(POOL_WIDTH),
                   tok(DA_WIDTH), tok(DA_WIDTH), tok(DA_WIDTH)],
        out_shape=[
            jax.ShapeDtypeStruct((T, 3 * DN_WIDTH), F32),
            jax.ShapeDtypeStruct((T, DN_WIDTH), F32),
            jax.ShapeDtypeStruct((T, V7X_LANES), F32),
            jax.ShapeDtypeStruct((T, POOL_WIDTH), F32),
            jax.ShapeDtypeStruct((T, DA_WIDTH), F32),
            jax.ShapeDtypeStruct((T, DA_WIDTH), F32),
            jax.ShapeDtypeStruct((T, DA_WIDTH), F32),
        ],
        compiler_params=_params("parallel"),
        name="mix_in",
    )(x, gain, wqkv, wz, wba, wpool, wda, cos, sin)


def _unit_tri_inverse(mats, eye, ri, ci, n):
    blk = 2
    same = (ri // blk) == (ci // blk)
    ts = [eye - jnp.where(same, a, 0.0) for a in mats]
    while blk < n:
        joins = ((ri // (2 * blk)) == (ci // (2 * blk))) & ((ri // blk) != (ci // blk))
        ets = [_mm(jnp.where(joins, a, 0.0), t) for a, t in zip(mats, ts)]
        ts = [t - _mm(t, et) for t, et in zip(ts, ets)]
        blk *= 2
    return ts


def _deltanet_kernel(alog_ref, dtb_ref, q_ref, k_ref, v_ref, z_ref, ba_ref, cq_ref, ck_ref, cv_ref,
                     onorm_ref, o_ref,
                     pad_ref, qn_ref, kn_ref, vn_ref, beta_ref, gc_ref, u_ref, wq_ref, qk_ref, kd_ref, gl_ref,
                     of_ref, ob_ref, *, S, C):
    assert C == V7X_LANES
    head = pl.program_id(1)
    RC = ROW_CHUNK
    HALO = V7X_SUBLANES
    n_rc = S // RC
    half = DN_CONV // 2

    def conv_phase(x_ref, w_ref, dst_ref, l2, scale):
        pad_ref[0:HALO, :] = jnp.zeros((HALO, DN_HEAD_DIM), F32)
        pad_ref[HALO + S:2 * HALO + S, :] = jnp.zeros((HALO, DN_HEAD_DIM), F32)
        pad_ref[HALO:HALO + S, :] = x_ref[0]
        w = w_ref[...]

        def body(i, carry):
            r0 = pl.multiple_of(i * RC, RC)
            win = pad_ref[pl.ds(r0, RC + 2 * HALO), :]
            acc = jnp.zeros((RC, DN_HEAD_DIM), F32)
            for j in range(DN_CONV):
                off = HALO + j - half
                acc = acc + win[off:off + RC, :] * w[j:j + 1, :]
            y = acc * _sigmoid(acc)
            if l2:
                y = y * lax.rsqrt(jnp.sum(y * y, axis=-1, keepdims=True) + L2_EPS)
            dst_ref[pl.ds(r0, RC), :] = y * scale
            return carry

        lax.fori_loop(0, n_rc, body, 0)

    conv_phase(q_ref, cq_ref, qn_ref, True, DN_HEAD_DIM ** -0.5)
    conv_phase(k_ref, ck_ref, kn_ref, True, 1.0)
    conv_phase(v_ref, cv_ref, vn_ref, False, 1.0)

    lane = lax.broadcasted_iota(jnp.int32, (1, V7X_LANES), 1)
    rr = lax.broadcasted_iota(jnp.int32, (RC, RC), 0)
    rc = lax.broadcasted_iota(jnp.int32, (RC, RC), 1)
    same_chunk = (rr // C) == (rc // C)
    cum_mat = [(same_chunk & (rr >= rc)).astype(BF16), (same_chunk & (rr <= rc)).astype(BF16)]

    def gate_body(i, carry):
        r0 = pl.multiple_of(i * RC, RC)
        ba = ba_ref[0, pl.ds(r0, RC), :]

        def col(c):
            return jnp.sum(jnp.where(lane == c, ba, 0.0), axis=-1, keepdims=True)

        for d in range(2):
            beta = _sigmoid(col(d * DN_HEADS + head))
            beta_ref[d, pl.ds(r0, RC), :] = jnp.broadcast_to(beta, (RC, DN_HEAD_DIM))
            a_raw = col(2 * DN_HEADS + d * DN_HEADS + head) + dtb_ref[d, head]
            softplus = jnp.maximum(a_raw, 0.0) + jnp.log(1.0 + jnp.exp(-jnp.abs(a_raw)))
            rate = jnp.exp(jnp.full((1, 1), alog_ref[d, head], F32))
            g = jnp.broadcast_to(-rate * softplus, (RC, DN_HEAD_DIM))
            g_hi = g.astype(BF16)
            g_mid, g_lo = _split16(g - g_hi.astype(F32))
            gc_ref[d, pl.ds(r0, RC), :] = (_dot(cum_mat[d], g_hi)
                                           + (_dot(cum_mat[d], g_mid) + _dot(cum_mat[d], g_lo)))
        return carry

    lax.fori_loop(0, n_rc, gate_body, 0)

    ri = lax.broadcasted_iota(jnp.int32, (C, C), 0)
    ci = lax.broadcasted_iota(jnp.int32, (C, C), 1)
    eye = (ri == ci).astype(F32)
    n_chunks = S // C

    def chunk_rows(c_idx, n):
        return pl.ds(pl.multiple_of(c_idx * n, n), n)

    def prepare(chains):
        st = []
        for d, c_idx in chains:
            incl = (ri >= ci) if d == 0 else (ri <= ci)
            rows = chunk_rows(c_idx, C)
            q = qn_ref[rows, :]
            k = kn_ref[rows, :]
            beta = beta_ref[d, rows, :]
            gc = gc_ref[d, rows, :]
            last = C - 1 if d == 0 else 0
            st.append(dict(
                d=d, c=c_idx, rows=rows, q=q, k=k, gc=gc, kb=k * beta, vb=vn_ref[rows, :] * beta,
                decay=jnp.where(incl, jnp.exp(gc - gc.T), 0.0),
                strict=(ri > ci) if d == 0 else (ri < ci),
                g_last=gc[last:last + 1, :]))
        kq = [_dot_nt(jnp.concatenate([s["kb"], s["q"]], axis=0).astype(BF16), s["k"].astype(BF16))
              for s in st]
        a = [jnp.where(s["strict"], m[:C] * s["decay"], 0.0) for s, m in zip(st, kq)]
        t = _unit_tri_inverse(a, eye, ri, ci, C)
        eg = [jnp.exp(s["gc"]) for s in st]
        sol = [_mm(ti, jnp.concatenate([s["vb"], s["kb"] * e], axis=1)) for ti, s, e in zip(t, st, eg)]
        for s, m, e, x in zip(st, kq, eg, sol):
            d, c_idx, rows = s["d"], s["c"], s["rows"]
            u_ref[d, rows, :] = x[:, :DN_HEAD_DIM]
            wq_ref[d, chunk_rows(c_idx, 2 * C), :] = jnp.concatenate(
                [x[:, DN_HEAD_DIM:], s["q"] * e], axis=0).astype(BF16)
            qk_ref[d, rows, :] = (m[C:] * s["decay"]).astype(BF16)
            kd_ref[d, rows, :] = (s["k"] * jnp.exp(s["g_last"] - s["gc"])).astype(BF16)
            gl_ref[d, chunk_rows(c_idx, V7X_SUBLANES), :] = jnp.broadcast_to(
                jnp.exp(s["g_last"]), (V7X_SUBLANES, DN_HEAD_DIM))

    def prepare_body(i, carry):
        prepare([(d, i * DN_PREP_CHUNKS + j) for j in range(DN_PREP_CHUNKS) for d in range(2)])
        return carry

    lax.fori_loop(0, n_chunks // DN_PREP_CHUNKS, prepare_body, 0)

    def chunk_body(c, states):
        idx = (c, n_chunks - 1 - c)
        s16 = [s.astype(BF16) for s in states]
        ws = [_dot(wq_ref[d, chunk_rows(idx[d], 2 * C), :], s16[d]) for d in range(2)]
        v_new = [(u_ref[d, chunk_rows(idx[d], C), :] - ws[d][:C]).astype(BF16) for d in range(2)]
        o = [ws[d][C:] + _dot(qk_ref[d, chunk_rows(idx[d], C), :], v_new[d]) for d in range(2)]
        of_ref[chunk_rows(idx[0], C), :] = o[0]
        ob_ref[chunk_rows(idx[1], C), :] = o[1]
        return tuple(
            states[d] * gl_ref[d, chunk_rows(idx[d], V7X_SUBLANES), :][0:1, :]
            + _dot_tn(kd_ref[d, chunk_rows(idx[d], C), :], v_new[d]) for d in range(2))

    zero_state = jnp.zeros((DN_HEAD_DIM, DN_HEAD_DIM), F32)
    lax.fori_loop(0, n_chunks, chunk_body, (zero_state, zero_state))

    def out_body(i, carry):
        r0 = pl.multiple_of(i * RC, RC)
        o = of_ref[pl.ds(r0, RC), :] + ob_ref[pl.ds(r0, RC), :]
        z = z_ref[0, pl.ds(r0, RC), :]
        y = _rms(o, onorm_ref[...]) * (z * _sigmoid(z))
        o_ref[0, pl.ds(r0, RC), :] = y.astype(o_ref.dtype)
        return carry

    lax.fori_loop(0, n_rc, out_body, 0)


def _deltanet(qkv, z, ba, conv_w, a_log, dt_bias, out_norm):
    B, S, _ = qkv.shape
    C = DN_CHUNK
    hd = DN_HEAD_DIM
    seq = lambda off: pl.BlockSpec((1, S, hd), lambda b, h: (b, 0, off + h))
    cw = lambda off: pl.BlockSpec((DN_CONV, hd), lambda b, h: (0, off + h))
    smem = pl.BlockSpec(memory_space=pltpu.SMEM)
    return pl.pallas_call(
        functools.partial(_deltanet_kernel, S=S, C=C),
        grid=(B, DN_HEADS),
        in_specs=[
            smem, smem,
            seq(0), seq(DN_HEADS), seq(2 * DN_HEADS),
            seq(0),
            pl.BlockSpec((1, S, V7X_LANES), lambda b, h: (b, 0, 0)),
            cw(0), cw(DN_HEADS), cw(2 * DN_HEADS),
            pl.BlockSpec((1, hd), lambda b, h: (0, 0)),
        ],
        out_specs=pl.BlockSpec((1, S, hd), lambda b, h: (b, 0, h)),
        out_shape=jax.ShapeDtypeStruct((B, S, DN_WIDTH), BF16),
        scratch_shapes=[
            pltpu.VMEM((S + 2 * V7X_SUBLANES, hd), F32),
            pltpu.VMEM((S, hd), F32), pltpu.VMEM((S, hd), F32), pltpu.VMEM((S, hd), F32),
            pltpu.VMEM((2, S, hd), F32), pltpu.VMEM((2, S, hd), F32),
            pltpu.VMEM((2, S, hd), F32),
            pltpu.VMEM((2, 2 * S, hd), BF16),
            pltpu.VMEM((2, S, C), BF16),
            pltpu.VMEM((2, S, hd), BF16),
            pltpu.VMEM((2, (S // C) * V7X_SUBLANES, hd), F32),
            pltpu.VMEM((S, hd), F32), pltpu.VMEM((S, hd), F32),
        ],
        compiler_params=_params("parallel", "parallel"),
        name="deltanet",
    )(a_log, dt_bias, qkv, qkv, qkv, z, ba, conv_w, conv_w, conv_w, out_norm)


def _pool_kernel(u_ref, w_ref, sc_ref, o_ref, pad_ref, *, S):
    grp = pl.program_id(1)
    RC = ROW_CHUNK
    HALO = 2 * V7X_SUBLANES
    gd = POOL_GROUP_DIM
    pad_ref[0:HALO, :] = jnp.zeros((HALO, gd), F32)
    pad_ref[HALO + S:2 * HALO + S, :] = jnp.zeros((HALO, gd), F32)
    pad_ref[HALO:HALO + S, :] = u_ref[0]
    w16 = w_ref[0].astype(BF16)
    scale = sc_ref[...]

    for gi, win in enumerate(POOL_WINDOWS):
        lo_off = -(win // 2)
        hi_off = win - win // 2

        @pl.when(grp == gi)
        def _():
            def body(i, carry):
                r0 = pl.multiple_of(i * RC, RC)
                blk = pad_ref[pl.ds(r0, RC + 2 * HALO), :]
                acc = jnp.zeros((RC, gd), F32)
                for off in range(lo_off, hi_off):
                    acc = acc + blk[HALO + off:HALO + off + RC, :]
                pos = r0 + lax.broadcasted_iota(jnp.int32, (RC, 1), 0)
                cnt = jnp.minimum(pos + hi_off, S) - jnp.maximum(pos + lo_off, 0)
                pooled = acc / cnt.astype(F32) - blk[HALO:HALO + RC, :]
                mixed = _dot(pooled.astype(BF16), w16) * scale
                o_ref[0, pl.ds(r0, RC), :] = mixed.astype(o_ref.dtype)
                return carry

            lax.fori_loop(0, S // RC, body, 0)


def _pool(u, pool_w, pool_scale):
    B, S, _ = u.shape
    gd = POOL_GROUP_DIM
    return pl.pallas_call(
        functools.partial(_pool_kernel, S=S),
        grid=(B, len(POOL_WINDOWS)),
        in_specs=[
            pl.BlockSpec((1, S, gd), lambda b, g: (b, 0, g)),
            pl.BlockSpec((1, gd, gd), lambda b, g: (g, 0, 0)),
            pl.BlockSpec((1, gd), lambda b, g: (0, g)),
        ],
        out_specs=pl.BlockSpec((1, S, gd), lambda b, g: (b, 0, g)),
        out_shape=jax.ShapeDtypeStruct((B, S, POOL_WIDTH), BF16),
        scratch_shapes=[pltpu.VMEM((S + 4 * V7X_SUBLANES, gd), F32)],
        compiler_params=_params("parallel", "parallel"),
        name="pool",
    )(u, pool_w, pool_scale)


def _dilated_kernel(q_ref, k_ref, v_ref, o_ref,
                    qd_ref, kd_ref, vd_ref, od_ref, ld_ref, on_ref, ln_ref, num_ref, den_ref, mx_ref, *, S):
    grp = pl.program_id(2)
    QB = DA_QBLOCK
    W = V7X_LANES
    n_it = S // QB
    lane = lax.broadcasted_iota(jnp.int32, (1, W), 1)
    pair_mask = [((lane // (DA_HEAD_DIM // 2)) % 2 == e) for e in range(2)]
    first_head = lane < DA_HEAD_DIM
    qi = lax.broadcasted_iota(jnp.int32, (QB, 2 * QB), 0)
    kj = lax.broadcasted_iota(jnp.int32, (QB, 2 * QB), 1)

    for gi, (window, dil) in enumerate(DA_CONFIGS):
        radius = window // (2 * dil)
        assert radius == QB // 2
        L = S // dil
        LP = L + QB
        blocks_per_res = L // QB
        band = jnp.abs(kj - radius - qi) <= radius

        @pl.when(grp == gi)
        def _():
            zpad = jnp.zeros((radius, W), BF16)
            for r in range(dil):
                src = pl.ds(r, L, stride=dil) if dil > 1 else pl.ds(0, L)
                qd_ref[r * L:(r + 1) * L, :] = q_ref[0, src, :].astype(BF16)
                base = r * LP
                kd_ref[base:base + radius, :] = zpad
                vd_ref[base:base + radius, :] = zpad
                kd_ref[base + radius:base + radius + L, :] = k_ref[0, src, :].astype(BF16)
                vd_ref[base + radius:base + radius + L, :] = v_ref[0, src, :].astype(BF16)
                kd_ref[base + radius + L:base + LP, :] = zpad
                vd_ref[base + radius + L:base + LP, :] = zpad

            def body(it, carry):
                res = it // blocks_per_res
                m0 = (it % blocks_per_res) * QB
                q0 = pl.multiple_of(it * QB, QB)
                k0 = pl.multiple_of(it * QB + res * QB, QB)
                kpos = kj + (m0 - radius)
                valid = band & (kpos >= 0) & (kpos < L)
                qp = qd_ref[pl.ds(q0, QB), :]
                kp = kd_ref[pl.ds(k0, 2 * QB), :]
                vp = vd_ref[pl.ds(k0, 2 * QB), :]
                outs, lses = [], []
                for e in range(2):
                    qm = jnp.where(pair_mask[e], qp, jnp.zeros_like(qp))
                    s = jnp.where(valid, _dot_nt(qm, kp), MASK_VALUE)
                    mx = jnp.max(s, axis=-1, keepdims=True)
                    pe = jnp.exp(s - mx)
                    den = jnp.sum(pe, axis=-1, keepdims=True)
                    outs.append(_dot(pe.astype(BF16), vp) / den)
                    lses.append(mx + jnp.log(den))
                od_ref[pl.ds(q0, QB), :] = jnp.where(first_head, outs[0], outs[1])
                ld_ref[pl.ds(q0, QB), :] = jnp.where(first_head, lses[0], lses[1])
                return carry

            lax.fori_loop(0, n_it, body, 0)

            for r in range(dil):
                dst = pl.ds(r, L, stride=dil) if dil > 1 else pl.ds(0, L)
                on_ref[dst, :] = od_ref[r * L:(r + 1) * L, :]
                ln_ref[dst, :] = ld_ref[r * L:(r + 1) * L, :]

    @pl.when(grp == 0)
    def _():
        num_ref[...] = on_ref[...]
        den_ref[...] = jnp.ones_like(den_ref)
        mx_ref[...] = ln_ref[...]

    @pl.when(grp > 0)
    def _():
        m_old = mx_ref[...]
        lse = ln_ref[...]
        m_new = jnp.maximum(m_old, lse)
        a_old = jnp.exp(m_old - m_new)
        a_new = jnp.exp(lse - m_new)
        num_ref[...] = num_ref[...] * a_old + on_ref[...] * a_new
        den_ref[...] = den_ref[...] * a_old + a_new
        mx_ref[...] = m_new

    @pl.when(grp == DA_NGROUPS - 1)
    def _():
        o_ref[0] = (num_ref[...] / den_ref[...]).astype(o_ref.dtype)


def _dilated(daq, dak, dav):
    B, S, _ = daq.shape
    W = V7X_LANES
    pairs = DA_GROUP_WIDTH // W
    max_dil = max(d for _, d in DA_CONFIGS)
    grp_spec = pl.BlockSpec((1, S, W), lambda b, p, g: (b, 0, g * pairs + p))
    full = lambda dt: pltpu.VMEM((S, W), dt)
    return pl.pallas_call(
        functools.partial(_dilated_kernel, S=S),
        grid=(B, pairs, DA_NGROUPS),
        in_specs=[grp_spec, grp_spec, grp_spec],
        out_specs=pl.BlockSpec((1, S, W), lambda b, p, g: (b, 0, p)),
        out_shape=jax.ShapeDtypeStruct((B, S, DA_GROUP_WIDTH), BF16),
        scratch_shapes=[
            full(BF16),
            pltpu.VMEM((S + max_dil * DA_QBLOCK, W), BF16),
            pltpu.VMEM((S + max_dil * DA_QBLOCK, W), BF16),
            full(F32), full(F32), full(F32), full(F32), full(F32), full(F32), full(F32),
        ],
        compiler_params=_params("parallel", "parallel", "arbitrary"),
        name="dilated",
    )(daq, dak, dav)


def _mix_out_kernel(x_ref, gain_ref, ya_ref, yb_ref, yc_ref, wg_ref, bg_ref, wa_ref, wb_ref, wc_ref,
                    wo_ref, o_ref):
    x = x_ref[...]
    D = x.shape[-1]
    h = _rms(x, gain_ref[...]).astype(BF16)
    merged = jnp.zeros(x.shape, F32)
    for i, (y_ref, w_ref) in enumerate(((ya_ref, wa_ref), (yb_ref, wb_ref), (yc_ref, wc_ref))):
        gate = _sigmoid(_dot(h, wg_ref[:, i * D:(i + 1) * D]) + bg_ref[:, i * D:(i + 1) * D])
        merged = merged + gate * _dot(y_ref[...], w_ref[...])
    o_ref[...] = x + _dot(merged.astype(BF16), wo_ref[...])


def _mix_out(x, gain, ya, yb, yc, wg, bg, wa, wb, wc, wo):
    T, D = x.shape
    tm = MIX_TOKEN_TILE
    tok = lambda n: pl.BlockSpec((tm, n), lambda i: (i, 0))
    return pl.pallas_call(
        _mix_out_kernel,
        grid=(T // tm,),
        in_specs=[
            tok(D), _resident((1, D)),
            tok(ya.shape[1]), tok(yb.shape[1]), tok(yc.shape[1]),
            _resident(wg.shape), _resident(bg.shape), _resident(wa.shape), _resident(wb.shape),
            _resident(wc.shape), _resident(wo.shape),
        ],
        out_specs=tok(D),
        out_shape=jax.ShapeDtypeStruct((T, D), F32),
        compiler_params=_params("parallel"),
        name="mix_out",
    )(x, gain, ya, yb, yc, wg, bg, wa, wb, wc, wo)


def _pair_layout_columns():
    half = DA_HEAD_DIM // 2
    order = []
    for g in range(DA_NGROUPS):
        for p in range(DA_HEADS_PER_GROUP // 2):
            for part in range(2):
                for e in range(2):
                    head = 2 * p + e
                    start = g * DA_GROUP_WIDTH + head * DA_HEAD_DIM + part * half
                    order.extend(range(start, start + half))
    return jnp.asarray(order, dtype=jnp.int32)


def _rope_tables(seq_len):
    half = DA_HEAD_DIM // 2
    inv_freq = ROPE_THETA ** (-jnp.arange(half, dtype=F32) / half)
    ang = jnp.arange(seq_len).astype(F32)[:, None] * inv_freq[None, :]
    cos = jnp.tile(jnp.cos(ang), (1, V7X_LANES // half))
    sin = jnp.sin(ang)
    sin = jnp.concatenate([-sin, -sin, sin, sin], axis=1)
    return cos, sin


def kernel(x, ffn1_norm, ffn1_w_gate, ffn1_w_up, ffn1_w_down, mix_norm, w_in, dn_conv, dn_a_log, dn_dt_bias, dn_out_norm, pool_w, pool_scale, w_proj_a, w_proj_b, w_proj_c, w_gate, b_gate, w_out, ffn2_norm, ffn2_w_gate, ffn2_w_up, ffn2_w_down, final_norm):
    B, S, D = x.shape
    depth = w_in.shape[0]
    T = B * S
    off_z = 3 * DN_WIDTH
    off_beta = off_z + DN_WIDTH
    off_pool = off_beta + 4 * DN_HEADS
    off_da = off_pool + POOL_WIDTH
    pair_cols = _pair_layout_columns()
    cos, sin = _rope_tables(S)
    fgain = final_norm.reshape(1, D)

    xt = x.reshape(T, D)
    for l in range(depth):
        xt = _ffn(xt, ffn1_norm[l].reshape(1, D), ffn1_w_gate[l].astype(BF16), ffn1_w_up[l].astype(BF16),
                  ffn1_w_down[l].astype(BF16), fgain, final_norm=False)

        w = w_in[l]
        w_da = w[:, off_da:]
        w_da = jnp.concatenate([w_da[:, :DA_WIDTH][:, pair_cols],
                                w_da[:, DA_WIDTH:2 * DA_WIDTH][:, pair_cols],
                                w_da[:, 2 * DA_WIDTH:]], axis=1)
        w_ba = jnp.pad(w[:, off_beta:off_pool], ((0, 0), (0, V7X_LANES - 4 * DN_HEADS)))
        qkv, z, ba, u, daq, dak, dav = _mix_in(
            xt, mix_norm[l].reshape(1, D), w[:, :off_z].astype(BF16), w[:, off_z:off_beta].astype(BF16),
            w_ba.astype(BF16), w[:, off_pool:off_da].astype(BF16), w_da.astype(BF16), cos, sin, S)

        y_a = _deltanet(qkv.reshape(B, S, -1), z.reshape(B, S, -1), ba.reshape(B, S, -1), dn_conv[l],
                        dn_a_log[l], dn_dt_bias[l], dn_out_norm[l].reshape(1, DN_HEAD_DIM))
        y_b = _pool(u.reshape(B, S, -1), pool_w[l], pool_scale[l].reshape(1, POOL_WIDTH))
        y_c = _dilated(daq.reshape(B, S, -1), dak.reshape(B, S, -1), dav.reshape(B, S, -1))

        xt = _mix_out(xt, mix_norm[l].reshape(1, D), y_a.reshape(T, -1), y_b.reshape(T, -1),
                      y_c.reshape(T, -1), w_gate[l].astype(BF16), b_gate[l].reshape(1, -1),
                      w_proj_a[l].astype(BF16), w_proj_b[l].astype(BF16), w_proj_c[l].astype(BF16),
                      w_out[l].astype(BF16))

        xt = _ffn(xt, ffn2_norm[l].reshape(1, D), ffn2_w_gate[l].astype(BF16), ffn2_w_up[l].astype(BF16),
                  ffn2_w_down[l].astype(BF16), fgain, final_norm=(l == depth - 1))
    return xt.reshape(B, S, D)
```

```python
import functools

import jax
import jax.numpy as jnp
from jax import lax
from jax.experimental import pallas as pl
from jax.experimental.pallas import tpu as pltpu

F32 = jnp.float32
BF16 = jnp.bfloat16

RMS_EPS = 1e-6
L2_EPS = 1e-6

DN_HEADS = 4
DN_HEAD_DIM = 128
DN_WIDTH = DN_HEADS * DN_HEAD_DIM
DN_CONV = 5
DN_CHUNK = 128
DN_PREP_CHUNKS = 8

POOL_WINDOWS = (2, 4, 8, 16)
POOL_GROUP_DIM = 128
POOL_WIDTH = len(POOL_WINDOWS) * POOL_GROUP_DIM

DA_CONFIGS = ((128, 1), (512, 4), (2048, 16))
DA_NGROUPS = len(DA_CONFIGS)
DA_HEADS_PER_GROUP = 4
DA_HEAD_DIM = 64
DA_GROUP_WIDTH = DA_HEADS_PER_GROUP * DA_HEAD_DIM
DA_WIDTH = DA_NGROUPS * DA_GROUP_WIDTH
DA_QBLOCK = 128
DA_LOCKSTEP = 4
ROPE_THETA = 10000.0
MASK_VALUE = -1e30

N_BRANCHES = 3

V7X_LANES = 128
V7X_SUBLANES = 8
V7X_VMEM_LIMIT_BYTES = 56 * 1024 * 1024

FFN_TOKEN_TILE = 512
MIX_TOKEN_TILE = 512
ROW_CHUNK = 256


def _params(*semantics):
    return pltpu.CompilerParams(dimension_semantics=semantics,
                                vmem_limit_bytes=V7X_VMEM_LIMIT_BYTES)


def _resident(shape):
    nd = len(shape)
    return pl.BlockSpec(shape, lambda *_: (0,) * nd, pipeline_mode=pl.Buffered(1))


def _rms(x, gain):
    return x * lax.rsqrt(jnp.mean(x * x, axis=-1, keepdims=True) + RMS_EPS) * gain


def _sigmoid(x):
    return 1.0 / (1.0 + jnp.exp(-x))


def _dot(a, b):
    return jnp.dot(a, b, preferred_element_type=F32)


def _dot_nt(a, b):
    return lax.dot_general(a, b, (((1,), (1,)), ((), ())), preferred_element_type=F32)


def _dot_tn(a, b):
    return lax.dot_general(a, b, (((0,), (0,)), ((), ())), preferred_element_type=F32)


def _split16(a):
    hi = a.astype(BF16)
    return hi, (a - hi.astype(F32)).astype(BF16)


def _mm(a, b):
    return _dot(a.astype(BF16), b.astype(BF16))


def _ffn_kernel(x_ref, gain_ref, wg_ref, wu_ref, wd_ref, fgain_ref, o_ref, h_ref, acc_ref, *, final_norm):
    j = pl.program_id(1)

    @pl.when(j == 0)
    def _():
        h_ref[...] = _rms(x_ref[...], gain_ref[...]).astype(BF16)
        acc_ref[...] = jnp.zeros_like(acc_ref)

    h = h_ref[...]
    g = _dot(h, wg_ref[...])
    u = _dot(h, wu_ref[...])
    a = (g * _sigmoid(g) * u).astype(BF16)
    acc_ref[...] += _dot(a, wd_ref[...])

    @pl.when(j == pl.num_programs(1) - 1)
    def _():
        y = x_ref[...] + 0.5 * acc_ref[...]
        if final_norm:
            y = _rms(y, fgain_ref[...])
        o_ref[...] = y


def _ffn(x, gain, wg, wu, wd, fgain, *, final_norm):
    T, D = x.shape
    F = wg.shape[1]
    tm = FFN_TOKEN_TILE
    fc = F // 2
    return pl.pallas_call(
        functools.partial(_ffn_kernel, final_norm=final_norm),
        grid=(T // tm, F // fc),
        in_specs=[
            pl.BlockSpec((tm, D), lambda i, j: (i, 0)),
            pl.BlockSpec((1, D), lambda i, j: (0, 0)),
            pl.BlockSpec((D, fc), lambda i, j: (0, j)),
            pl.BlockSpec((D, fc), lambda i, j: (0, j)),
            pl.BlockSpec((fc, D), lambda i, j: (j, 0)),
            pl.BlockSpec((1, D), lambda i, j: (0, 0)),
        ],
        out_specs=pl.BlockSpec((tm, D), lambda i, j: (i, 0)),
        out_shape=jax.ShapeDtypeStruct((T, D), F32),
        scratch_shapes=[pltpu.VMEM((tm, D), BF16), pltpu.VMEM((tm, D), F32)],
        compiler_params=_params("parallel", "arbitrary"),
        name="ffn",
    )(x, gain, wg, wu, wd, fgain)


def _mix_in_kernel(x_ref, gain_ref, wqkv_ref, wz_ref, wba_ref, wpool_ref, wda_ref, cos_ref, sin_ref,
                   qkv_ref, z_ref, ba_ref, u_ref, daq_ref, dak_ref, dav_ref):
    h = _rms(x_ref[...], gain_ref[...]).astype(BF16)
    qkv_ref[...] = _dot(h, wqkv_ref[...])
    z_ref[...] = _dot(h, wz_ref[...])
    ba_ref[...] = _dot(h, wba_ref[...])
    u_ref[...] = _dot(h, wpool_ref[...])
    da = _dot(h, wda_ref[...])
    cos = cos_ref[...]
    sin = sin_ref[...]
    for part, out_ref, scale in ((0, daq_ref, DA_HEAD_DIM ** -0.5), (1, dak_ref, 1.0)):
        for cb in range(DA_WIDTH // V7X_LANES):
            lo = part * DA_WIDTH + cb * V7X_LANES
            t = da[:, lo:lo + V7X_LANES]
            r = t * cos + pltpu.roll(t, V7X_LANES // 2, 1) * sin
            out_ref[:, cb * V7X_LANES:(cb + 1) * V7X_LANES] = r * scale
    dav_ref[...] = da[:, 2 * DA_WIDTH:]


def _mix_in(x, gain, wqkv, wz, wba, wpool, wda, cos, sin, seq_len):
    T, D = x.shape
    tm = MIX_TOKEN_TILE
    tiles_per_seq = seq_len // tm
    tok = lambda n: pl.BlockSpec((tm, n), lambda i: (i, 0))
    return pl.pallas_call(
        _mix_in_kernel,
        grid=(T // tm,),
        in_specs=[
            tok(D),
            _resident((1, D)),
            _resident(wqkv.shape), _resident(wz.shape), _resident(wba.shape),
            _resident(wpool.shape), _resident(wda.shape),
            pl.BlockSpec((tm, V7X_LANES), lambda i: (i % tiles_per_seq, 0)),
            pl.BlockSpec((tm, V7X_LANES), lambda i: (i % tiles_per_seq, 0)),
        ],
        out_specs=[tok(3 * DN_WIDTH), tok(DN_WIDTH), tok(V7X_LANES), tok(POOL_WIDTH),
                   tok(DA_WIDTH), tok(DA_WIDTH), tok(DA_WIDTH)],
        out_shape=[
            jax.ShapeDtypeStruct((T, 3 * DN_WIDTH), F32),
            jax.ShapeDtypeStruct((T, DN_WIDTH), F32),
            jax.ShapeDtypeStruct((T, V7X_LANES), F32),
            jax.ShapeDtypeStruct((T, POOL_WIDTH), F32),
            jax.ShapeDtypeStruct((T, DA_WIDTH), F32),
            jax.ShapeDtypeStruct((T, DA_WIDTH), F32),
            jax.ShapeDtypeStruct((T, DA_WIDTH), F32),
        ],
        compiler_params=_params("parallel"),
        name="mix_in",
    )(x, gain, wqkv, wz, wba, wpool, wda, cos, sin)


def _unit_tri_inverse(mats, eye, ri, ci, n):
    blk = 2
    same = (ri // blk) == (ci // blk)
    ts = [eye - jnp.where(same, a, 0.0) for a in mats]
    while blk < n:
        joins = ((ri // (2 * blk)) == (ci // (2 * blk))) & ((ri // blk) != (ci // blk))
        ets = [_mm(jnp.where(joins, a, 0.0), t) for a, t in zip(mats, ts)]
        ts = [t - _mm(t, et) for t, et in zip(ts, ets)]
        blk *= 2
    return ts


def _deltanet_kernel(alog_ref, dtb_ref, q_ref, k_ref, v_ref, z_ref, ba_ref, cq_ref, ck_ref, cv_ref,
                     onorm_ref, o_ref,
                     pad_ref, qn_ref, kn_ref, vn_ref, beta_ref, gc_ref, u_ref, wq_ref, qk_ref, kd_ref, gl_ref,
                     of_ref, ob_ref, *, S, C):
    assert C == V7X_LANES
    head = pl.program_id(1)
    RC = ROW_CHUNK
    HALO = V7X_SUBLANES
    n_rc = S // RC
    half = DN_CONV // 2

    def conv_phase(x_ref, w_ref, dst_ref, l2, scale):
        pad_ref[0:HALO, :] = jnp.zeros((HALO, DN_HEAD_DIM), F32)
        pad_ref[HALO + S:2 * HALO + S, :] = jnp.zeros((HALO, DN_HEAD_DIM), F32)
        pad_ref[HALO:HALO + S, :] = x_ref[0]
        w = w_ref[...]

        def body(i, carry):
            r0 = pl.multiple_of(i * RC, RC)
            win = pad_ref[pl.ds(r0, RC + 2 * HALO), :]
            acc = jnp.zeros((RC, DN_HEAD_DIM), F32)
            for j in range(DN_CONV):
                off = HALO + j - half
                acc = acc + win[off:off + RC, :] * w[j:j + 1, :]
            y = acc * _sigmoid(acc)
            if l2:
                y = y * lax.rsqrt(jnp.sum(y * y, axis=-1, keepdims=True) + L2_EPS)
            dst_ref[pl.ds(r0, RC), :] = y * scale
            return carry

        lax.fori_loop(0, n_rc, body, 0)

    conv_phase(q_ref, cq_ref, qn_ref, True, DN_HEAD_DIM ** -0.5)
    conv_phase(k_ref, ck_ref, kn_ref, True, 1.0)
    conv_phase(v_ref, cv_ref, vn_ref, False, 1.0)

    lane = lax.broadcasted_iota(jnp.int32, (1, V7X_LANES), 1)
    rr = lax.broadcasted_iota(jnp.int32, (RC, RC), 0)
    rc = lax.broadcasted_iota(jnp.int32, (RC, RC), 1)
    same_chunk = (rr // C) == (rc // C)
    cum_mat = [(same_chunk & (rr >= rc)).astype(BF16), (same_chunk & (rr <= rc)).astype(BF16)]

    GRP = 4 * DN_HEADS
    rate_row = jnp.exp(alog_ref[...])
    dtb_row = dtb_ref[...]

    def gate_body(i, carry):
        r0 = pl.multiple_of(i * RC, RC)
        ba = ba_ref[0, pl.ds(r0, RC), :]
        sig = _sigmoid(ba)
        a_raw = ba + dtb_row
        softplus = jnp.maximum(a_raw, 0.0) + jnp.log(1.0 + jnp.exp(-jnp.abs(a_raw)))
        g = -rate_row * softplus
        g_hi = g.astype(BF16).astype(F32)
        g_mid = (g - g_hi).astype(BF16).astype(F32)
        g_lo = (g - g_hi) - g_mid
        parts = jnp.where(lane < GRP, g_hi,
                          jnp.where(lane < 2 * GRP, pltpu.roll(g_mid, GRP, 1),
                                    jnp.where(lane < 3 * GRP, pltpu.roll(g_lo, 2 * GRP, 1), 0.0))).astype(BF16)

        def pick(x, sel):
            col = jnp.sum(jnp.where(sel, x, 0.0), axis=-1, keepdims=True)
            return jnp.broadcast_to(col, (RC, DN_HEAD_DIM))

        for d in range(2):
            cum = _dot(cum_mat[d], parts)
            beta_ref[d, pl.ds(r0, RC), :] = pick(sig, lane == d * DN_HEADS + head)
            c = 2 * DN_HEADS + d * DN_HEADS + head
            gc_ref[d, pl.ds(r0, RC), :] = pick(cum, (lane == c) | (lane == c + GRP) | (lane == c + 2 * GRP))
        return carry

    lax.fori_loop(0, n_rc, gate_body, 0, unroll=2)

    ri = lax.broadcasted_iota(jnp.int32, (C, C), 0)
    ci = lax.broadcasted_iota(jnp.int32, (C, C), 1)
    eye = (ri == ci).astype(F32)
    n_chunks = S // C

    def chunk_rows(c_idx, n):
        return pl.ds(pl.multiple_of(c_idx * n, n), n)

    def prepare(chains):
        st = []
        for d, c_idx in chains:
            incl = (ri >= ci) if d == 0 else (ri <= ci)
            rows = chunk_rows(c_idx, C)
            q = qn_ref[rows, :]
            k = kn_ref[rows, :]
            beta = beta_ref[d, rows, :]
            gc = gc_ref[d, rows, :]
            last = C - 1 if d == 0 else 0
            st.append(dict(
                d=d, c=c_idx, rows=rows, q=q, k=k, gc=gc, kb=k * beta, vb=vn_ref[rows, :] * beta,
                decay=jnp.where(incl, jnp.exp(gc - gc.T), 0.0),
                strict=(ri > ci) if d == 0 else (ri < ci),
                g_last=gc[last:last + 1, :]))
        kq = [_dot_nt(jnp.concatenate([s["kb"], s["q"]], axis=0).astype(BF16), s["k"].astype(BF16))
              for s in st]
        a = [jnp.where(s["strict"], m[:C] * s["decay"], 0.0) for s, m in zip(st, kq)]
        t = _unit_tri_inverse(a, eye, ri, ci, C)
        eg = [jnp.exp(s["gc"]) for s in st]
        sol = [_mm(ti, jnp.concatenate([s["vb"], s["kb"] * e], axis=1)) for ti, s, e in zip(t, st, eg)]
        for s, m, e, x in zip(st, kq, eg, sol):
            d, c_idx, rows = s["d"], s["c"], s["rows"]
            u_ref[d, rows, :] = x[:, :DN_HEAD_DIM]
            wq_ref[d, chunk_rows(c_idx, 2 * C), :] = jnp.concatenate(
                [x[:, DN_HEAD_DIM:], s["q"] * e], axis=0).astype(BF16)
            qk_ref[d, rows, :] = (m[C:] * s["decay"]).astype(BF16)
            kd_ref[d, rows, :] = (s["k"] * jnp.exp(s["g_last"] - s["gc"])).astype(BF16)
            gl_ref[d, chunk_rows(c_idx, V7X_SUBLANES), :] = jnp.broadcast_to(
                jnp.exp(s["g_last"]), (V7X_SUBLANES, DN_HEAD_DIM))

    def prepare_body(i, carry):
        prepare([(d, i * DN_PREP_CHUNKS + j) for j in range(DN_PREP_CHUNKS) for d in range(2)])
        return carry

    lax.fori_loop(0, n_chunks // DN_PREP_CHUNKS, prepare_body, 0)

    def chunk_body(c, states):
        idx = (c, n_chunks - 1 - c)
        s16 = [s.astype(BF16) for s in states]
        ws = [_dot(wq_ref[d, chunk_rows(idx[d], 2 * C), :], s16[d]) for d in range(2)]
        v_new = [(u_ref[d, chunk_rows(idx[d], C), :] - ws[d][:C]).astype(BF16) for d in range(2)]
        o = [ws[d][C:] + _dot(qk_ref[d, chunk_rows(idx[d], C), :], v_new[d]) for d in range(2)]
        of_ref[chunk_rows(idx[0], C), :] = o[0]
        ob_ref[chunk_rows(idx[1], C), :] = o[1]
        return tuple(
            states[d] * gl_ref[d, chunk_rows(idx[d], V7X_SUBLANES), :][0:1, :]
            + _dot_tn(kd_ref[d, chunk_rows(idx[d], C), :], v_new[d]) for d in range(2))

    zero_state = jnp.zeros((DN_HEAD_DIM, DN_HEAD_DIM), F32)
    lax.fori_loop(0, n_chunks, chunk_body, (zero_state, zero_state))

    def out_body(i, carry):
        r0 = pl.multiple_of(i * RC, RC)
        o = of_ref[pl.ds(r0, RC), :] + ob_ref[pl.ds(r0, RC), :]
        z = z_ref[0, pl.ds(r0, RC), :]
        y = _rms(o, onorm_ref[...]) * (z * _sigmoid(z))
        o_ref[0, pl.ds(r0, RC), :] = y.astype(o_ref.dtype)
        return carry

    lax.fori_loop(0, n_rc, out_body, 0)


def _deltanet(qkv, z, ba, conv_w, a_log, dt_bias, out_norm):
    B, S, _ = qkv.shape
    C = DN_CHUNK
    hd = DN_HEAD_DIM
    seq = lambda off: pl.BlockSpec((1, S, hd), lambda b, h: (b, 0, off + h))
    cw = lambda off: pl.BlockSpec((DN_CONV, hd), lambda b, h: (0, off + h))
    row = pl.BlockSpec((1, V7X_LANES), lambda b, h: (0, 0))

    def lane_row(p):
        return jnp.pad(p.reshape(1, -1), ((0, 0), (2 * DN_HEADS, V7X_LANES - 4 * DN_HEADS)))

    a_log, dt_bias = lane_row(a_log), lane_row(dt_bias)
    return pl.pallas_call(
        functools.partial(_deltanet_kernel, S=S, C=C),
        grid=(B, DN_HEADS),
        in_specs=[
            row, row,
            seq(0), seq(DN_HEADS), seq(2 * DN_HEADS),
            seq(0),
            pl.BlockSpec((1, S, V7X_LANES), lambda b, h: (b, 0, 0)),
            cw(0), cw(DN_HEADS), cw(2 * DN_HEADS),
            pl.BlockSpec((1, hd), lambda b, h: (0, 0)),
        ],
        out_specs=pl.BlockSpec((1, S, hd), lambda b, h: (b, 0, h)),
        out_shape=jax.ShapeDtypeStruct((B, S, DN_WIDTH), BF16),
        scratch_shapes=[
            pltpu.VMEM((S + 2 * V7X_SUBLANES, hd), F32),
            pltpu.VMEM((S, hd), F32), pltpu.VMEM((S, hd), F32), pltpu.VMEM((S, hd), F32),
            pltpu.VMEM((2, S, hd), F32), pltpu.VMEM((2, S, hd), F32),
            pltpu.VMEM((2, S, hd), F32),
            pltpu.VMEM((2, 2 * S, hd), BF16),
            pltpu.VMEM((2, S, C), BF16),
            pltpu.VMEM((2, S, hd), BF16),
            pltpu.VMEM((2, (S // C) * V7X_SUBLANES, hd), F32),
            pltpu.VMEM((S, hd), F32), pltpu.VMEM((S, hd), F32),
        ],
        compiler_params=_params("parallel", "parallel"),
        name="deltanet",
    )(a_log, dt_bias, qkv, qkv, qkv, z, ba, conv_w, conv_w, conv_w, out_norm)


def _pool_kernel(u_ref, w_ref, sc_ref, o_ref, pad_ref, *, S):
    grp = pl.program_id(1)
    RC = ROW_CHUNK
    HALO = 2 * V7X_SUBLANES
    gd = POOL_GROUP_DIM
    pad_ref[0:HALO, :] = jnp.zeros((HALO, gd), F32)
    pad_ref[HALO + S:2 * HALO + S, :] = jnp.zeros((HALO, gd), F32)
    pad_ref[HALO:HALO + S, :] = u_ref[0]
    w16 = w_ref[0].astype(BF16)
    scale = sc_ref[...]

    for gi, win in enumerate(POOL_WINDOWS):
        lo_off = -(win // 2)
        hi_off = win - win // 2

        @pl.when(grp == gi)
        def _():
            def body(i, carry):
                r0 = pl.multiple_of(i * RC, RC)
                blk = pad_ref[pl.ds(r0, RC + 2 * HALO), :]
                acc = jnp.zeros((RC, gd), F32)
                for off in range(lo_off, hi_off):
                    acc = acc + blk[HALO + off:HALO + off + RC, :]
                pos = r0 + lax.broadcasted_iota(jnp.int32, (RC, 1), 0)
                cnt = jnp.minimum(pos + hi_off, S) - jnp.maximum(pos + lo_off, 0)
                pooled = acc / cnt.astype(F32) - blk[HALO:HALO + RC, :]
                mixed = _dot(pooled.astype(BF16), w16) * scale
                o_ref[0, pl.ds(r0, RC), :] = mixed.astype(o_ref.dtype)
                return carry

            lax.fori_loop(0, S // RC, body, 0)


def _pool(u, pool_w, pool_scale):
    B, S, _ = u.shape
    gd = POOL_GROUP_DIM
    return pl.pallas_call(
        functools.partial(_pool_kernel, S=S),
        grid=(B, len(POOL_WINDOWS)),
        in_specs=[
            pl.BlockSpec((1, S, gd), lambda b, g: (b, 0, g)),
            pl.BlockSpec((1, gd, gd), lambda b, g: (g, 0, 0)),
            pl.BlockSpec((1, gd), lambda b, g: (0, g)),
        ],
        out_specs=pl.BlockSpec((1, S, gd), lambda b, g: (b, 0, g)),
        out_shape=jax.ShapeDtypeStruct((B, S, POOL_WIDTH), BF16),
        scratch_shapes=[pltpu.VMEM((S + 4 * V7X_SUBLANES, gd), F32)],
        compiler_params=_params("parallel", "parallel"),
        name="pool",
    )(u, pool_w, pool_scale)


def _dilated_kernel(q_ref, k_ref, v_ref, o_ref,
                    qd_ref, kd_ref, vd_ref, od_ref, ld_ref, on_ref, ln_ref, num_ref, den_ref, mx_ref, *, S):
    grp = pl.program_id(2)
    QB = DA_QBLOCK
    W = V7X_LANES
    n_it = S // QB
    lane = lax.broadcasted_iota(jnp.int32, (1, W), 1)
    pair_mask = [((lane // (DA_HEAD_DIM // 2)) % 2 == e) for e in range(2)]
    first_head = lane < DA_HEAD_DIM
    qi2 = lax.broadcasted_iota(jnp.int32, (2 * QB, 2 * QB), 0) % QB
    kj2 = lax.broadcasted_iota(jnp.int32, (2 * QB, 2 * QB), 1)

    for gi, (window, dil) in enumerate(DA_CONFIGS):
        radius = window // (2 * dil)
        assert radius == QB // 2
        L = S // dil
        LP = L + QB
        blocks_per_res = L // QB
        band2 = jnp.abs(kj2 - radius - qi2) <= radius

        @pl.when(grp == gi)
        def _():
            zpad = jnp.zeros((radius, W), BF16)
            for r in range(dil):
                src = pl.ds(r, L, stride=dil) if dil > 1 else pl.ds(0, L)
                qd_ref[r * L:(r + 1) * L, :] = q_ref[0, src, :].astype(BF16)
                base = r * LP
                kd_ref[base:base + radius, :] = zpad
                vd_ref[base:base + radius, :] = zpad
                kd_ref[base + radius:base + radius + L, :] = k_ref[0, src, :].astype(BF16)
                vd_ref[base + radius:base + radius + L, :] = v_ref[0, src, :].astype(BF16)
                kd_ref[base + radius + L:base + LP, :] = zpad
                vd_ref[base + radius + L:base + LP, :] = zpad

            def body(step, carry):
                blocks = []
                for j in range(DA_LOCKSTEP):
                    it = step * DA_LOCKSTEP + j
                    res = it // blocks_per_res
                    m0 = (it % blocks_per_res) * QB
                    q0 = pl.multiple_of(it * QB, QB)
                    k0 = pl.multiple_of(it * QB + res * QB, QB)
                    kpos = kj2 + (m0 - radius)
                    qp = qd_ref[pl.ds(q0, QB), :]
                    zero = jnp.zeros_like(qp)
                    blocks.append(dict(
                        q0=q0, valid=band2 & (kpos >= 0) & (kpos < L),
                        q=jnp.concatenate([jnp.where(pair_mask[e], qp, zero) for e in range(2)], axis=0),
                        k=kd_ref[pl.ds(k0, 2 * QB), :], v=vd_ref[pl.ds(k0, 2 * QB), :]))
                scores = [jnp.where(b["valid"], _dot_nt(b["q"], b["k"]), MASK_VALUE) for b in blocks]
                mxs = [jnp.max(s, axis=-1, keepdims=True) for s in scores]
                probs = [jnp.exp(s - m) for s, m in zip(scores, mxs)]
                dens = [jnp.sum(p, axis=-1, keepdims=True) for p in probs]
                outs = [_dot(p.astype(BF16), b["v"]) / d for p, b, d in zip(probs, blocks, dens)]
                for b, o, m, d in zip(blocks, outs, mxs, dens):
                    lse = m + jnp.log(d)
                    od_ref[pl.ds(b["q0"], QB), :] = jnp.where(first_head, o[:QB], o[QB:])
                    ld_ref[pl.ds(b["q0"], QB), :] = jnp.where(first_head, lse[:QB], lse[QB:])
                return carry

            lax.fori_loop(0, n_it // DA_LOCKSTEP, body, 0)

            for r in range(dil):
                dst = pl.ds(r, L, stride=dil) if dil > 1 else pl.ds(0, L)
                on_ref[dst, :] = od_ref[r * L:(r + 1) * L, :]
                ln_ref[dst, :] = ld_ref[r * L:(r + 1) * L, :]

    @pl.when(grp == 0)
    def _():
        num_ref[...] = on_ref[...]
        den_ref[...] = jnp.ones_like(den_ref)
        mx_ref[...] = ln_ref[...]

    @pl.when(grp > 0)
    def _():
        m_old = mx_ref[...]
        lse = ln_ref[...]
        m_new = jnp.maximum(m_old, lse)
        a_old = jnp.exp(m_old - m_new)
        a_new = jnp.exp(lse - m_new)
        num_ref[...] = num_ref[...] * a_old + on_ref[...] * a_new
        den_ref[...] = den_ref[...] * a_old + a_new
        mx_ref[...] = m_new

    @pl.when(grp == DA_NGROUPS - 1)
    def _():
        o_ref[0] = (num_ref[...] / den_ref[...]).astype(o_ref.dtype)


def _dilated(daq, dak, dav):
    B, S, _ = daq.shape
    W = V7X_LANES
    pairs = DA_GROUP_WIDTH // W
    max_dil = max(d for _, d in DA_CONFIGS)
    grp_spec = pl.BlockSpec((1, S, W), lambda b, p, g: (b, 0, g * pairs + p))
    full = lambda dt: pltpu.VMEM((S, W), dt)
    return pl.pallas_call(
        functools.partial(_dilated_kernel, S=S),
        grid=(B, pairs, DA_NGROUPS),
        in_specs=[grp_spec, grp_spec, grp_spec],
        out_specs=pl.BlockSpec((1, S, W), lambda b, p, g: (b, 0, p)),
        out_shape=jax.ShapeDtypeStruct((B, S, DA_GROUP_WIDTH), BF16),
        scratch_shapes=[
            full(BF16),
            pltpu.VMEM((S + max_dil * DA_QBLOCK, W), BF16),
            pltpu.VMEM((S + max_dil * DA_QBLOCK, W), BF16),
            full(F32), full(F32), full(F32), full(F32), full(F32), full(F32), full(F32),
        ],
        compiler_params=_params("parallel", "parallel", "arbitrary"),
        name="dilated",
    )(daq, dak, dav)


def _mix_out_kernel(x_ref, gain_ref, ya_ref, yb_ref, yc_ref, wg_ref, bg_ref, wa_ref, wb_ref, wc_ref,
                    wo_ref, o_ref):
    x = x_ref[...]
    D = x.shape[-1]
    h = _rms(x, gain_ref[...]).astype(BF16)
    merged = jnp.zeros(x.shape, F32)
    for i, (y_ref, w_ref) in enumerate(((ya_ref, wa_ref), (yb_ref, wb_ref), (yc_ref, wc_ref))):
        gate = _sigmoid(_dot(h, wg_ref[:, i * D:(i + 1) * D]) + bg_ref[:, i * D:(i + 1) * D])
        merged = merged + gate * _dot(y_ref[...], w_ref[...])
    o_ref[...] = x + _dot(merged.astype(BF16), wo_ref[...])


def _mix_out(x, gain, ya, yb, yc, wg, bg, wa, wb, wc, wo):
    T, D = x.shape
    tm = MIX_TOKEN_TILE
    tok = lambda n: pl.BlockSpec((tm, n), lambda i: (i, 0))
    return pl.pallas_call(
        _mix_out_kernel,
        grid=(T // tm,),
        in_specs=[
            tok(D), _resident((1, D)),
            tok(ya.shape[1]), tok(yb.shape[1]), tok(yc.shape[1]),
            _resident(wg.shape), _resident(bg.shape), _resident(wa.shape), _resident(wb.shape),
            _resident(wc.shape), _resident(wo.shape),
        ],
        out_specs=tok(D),
        out_shape=jax.ShapeDtypeStruct((T, D), F32),
        compiler_params=_params("parallel"),
        name="mix_out",
    )(x, gain, ya, yb, yc, wg, bg, wa, wb, wc, wo)


def _pair_layout_columns():
    half = DA_HEAD_DIM // 2
    order = []
    for g in range(DA_NGROUPS):
        for p in range(DA_HEADS_PER_GROUP // 2):
            for part in range(2):
                for e in range(2):
                    head = 2 * p + e
                    start = g * DA_GROUP_WIDTH + head * DA_HEAD_DIM + part * half
                    order.extend(range(start, start + half))
    return jnp.asarray(order, dtype=jnp.int32)


def _rope_tables(seq_len):
    half = DA_HEAD_DIM // 2
    inv_freq = ROPE_THETA ** (-jnp.arange(half, dtype=F32) / half)
    ang = jnp.arange(seq_len).astype(F32)[:, None] * inv_freq[None, :]
    cos = jnp.tile(jnp.cos(ang), (1, V7X_LANES // half))
    sin = jnp.sin(ang)
    sin = jnp.concatenate([-sin, -sin, sin, sin], axis=1)
    return cos, sin


def kernel(x, ffn1_norm, ffn1_w_gate, ffn1_w_up, ffn1_w_down, mix_norm, w_in, dn_conv, dn_a_log, dn_dt_bias, dn_out_norm, pool_w, pool_scale, w_proj_a, w_proj_b, w_proj_c, w_gate, b_gate, w_out, ffn2_norm, ffn2_w_gate, ffn2_w_up, ffn2_w_down, final_norm):
    B, S, D = x.shape
    depth = w_in.shape[0]
    T = B * S
    off_z = 3 * DN_WIDTH
    off_beta = off_z + DN_WIDTH
    off_pool = off_beta + 4 * DN_HEADS
    off_da = off_pool + POOL_WIDTH
    pair_cols = _pair_layout_columns()
    cos, sin = _rope_tables(S)
    fgain = final_norm.reshape(1, D)

    xt = x.reshape(T, D)
    for l in range(depth):
        xt = _ffn(xt, ffn1_norm[l].reshape(1, D), ffn1_w_gate[l].astype(BF16), ffn1_w_up[l].astype(BF16),
                  ffn1_w_down[l].astype(BF16), fgain, final_norm=False)

        w = w_in[l]
        w_da = w[:, off_da:]
        w_da = jnp.concatenate([w_da[:, :DA_WIDTH][:, pair_cols],
                                w_da[:, DA_WIDTH:2 * DA_WIDTH][:, pair_cols],
                                w_da[:, 2 * DA_WIDTH:]], axis=1)
        w_ba = jnp.pad(w[:, off_beta:off_pool], ((0, 0), (0, V7X_LANES - 4 * DN_HEADS)))
        qkv, z, ba, u, daq, dak, dav = _mix_in(
            xt, mix_norm[l].reshape(1, D), w[:, :off_z].astype(BF16), w[:, off_z:off_beta].astype(BF16),
            w_ba.astype(BF16), w[:, off_pool:off_da].astype(BF16), w_da.astype(BF16), cos, sin, S)

        y_a = _deltanet(qkv.reshape(B, S, -1), z.reshape(B, S, -1), ba.reshape(B, S, -1), dn_conv[l],
                        dn_a_log[l], dn_dt_bias[l], dn_out_norm[l].reshape(1, DN_HEAD_DIM))
        y_b = _pool(u.reshape(B, S, -1), pool_w[l], pool_scale[l].reshape(1, POOL_WIDTH))
        y_c = _dilated(daq.reshape(B, S, -1), dak.reshape(B, S, -1), dav.reshape(B, S, -1))

        xt = _mix_out(xt, mix_norm[l].reshape(1, D), y_a.reshape(T, -1), y_b.reshape(T, -1),
                      y_c.reshape(T, -1), w_gate[l].astype(BF16), b_gate[l].reshape(1, -1),
                      w_proj_a[l].astype(BF16), w_proj_b[l].astype(BF16), w_proj_c[l].astype(BF16),
                      w_out[l].astype(BF16))

        xt = _ffn(xt, ffn2_norm[l].reshape(1, D), ffn2_w_gate[l].astype(BF16), ffn2_w_up[l].astype(BF16),
                  ffn2_w_down[l].astype(BF16), fgain, final_norm=(l == depth - 1))
    return xt.reshape(B, S, D)
```

```python
import functools

import jax
import jax.numpy as jnp
from jax import lax
from jax.experimental import pallas as pl
from jax.experimental.pallas import tpu as pltpu

F32 = jnp.float32
BF16 = jnp.bfloat16

RMS_EPS = 1e-6
L2_EPS = 1e-6

DN_HEADS = 4
DN_HEAD_DIM = 128
DN_WIDTH = DN_HEADS * DN_HEAD_DIM
DN_CONV = 5
DN_CHUNK = 128
DN_PREP_CHUNKS = 8

POOL_WINDOWS = (2, 4, 8, 16)
POOL_GROUP_DIM = 128
POOL_WIDTH = len(POOL_WINDOWS) * POOL_GROUP_DIM

DA_CONFIGS = ((128, 1), (512, 4), (2048, 16))
DA_NGROUPS = len(DA_CONFIGS)
DA_HEADS_PER_GROUP = 4
DA_HEAD_DIM = 64
DA_GROUP_WIDTH = DA_HEADS_PER_GROUP * DA_HEAD_DIM
DA_WIDTH = DA_NGROUPS * DA_GROUP_WIDTH
DA_QBLOCK = 128
DA_LOCKSTEP = 4
ROPE_THETA = 10000.0
MASK_VALUE = -1e30

N_BRANCHES = 3

V7X_LANES = 128
V7X_SUBLANES = 8
V7X_VMEM_LIMIT_BYTES = 56 * 1024 * 1024

FFN_TOKEN_TILE = 512
FFN_HIDDEN_CHUNK = 1408
MIX_TOKEN_TILE = 512
ROW_CHUNK = 256


def _params(*semantics):
    return pltpu.CompilerParams(dimension_semantics=semantics,
                                vmem_limit_bytes=V7X_VMEM_LIMIT_BYTES)


def _resident(shape):
    nd = len(shape)
    return pl.BlockSpec(shape, lambda *_: (0,) * nd, pipeline_mode=pl.Buffered(1))


def _rms(x, gain):
    return x * lax.rsqrt(jnp.mean(x * x, axis=-1, keepdims=True) + RMS_EPS) * gain


def _sigmoid(x):
    return 1.0 / (1.0 + jnp.exp(-x))


def _dot(a, b):
    return jnp.dot(a, b, preferred_element_type=F32)


def _dot_nt(a, b):
    return lax.dot_general(a, b, (((1,), (1,)), ((), ())), preferred_element_type=F32)


def _dot_tn(a, b):
    return lax.dot_general(a, b, (((0,), (0,)), ((), ())), preferred_element_type=F32)


def _mm(a, b):
    return _dot(a.astype(BF16), b.astype(BF16))


def _ffn_kernel(x_ref, gain_ref, wg_ref, wu_ref, wd_ref, fgain_ref, o_ref, *, final_norm):
    x = x_ref[...]
    h = _rms(x, gain_ref[...]).astype(BF16)
    F = wg_ref.shape[1]
    acc = jnp.zeros(x.shape, F32)
    for c in range(F // FFN_HIDDEN_CHUNK):
        cols = slice(c * FFN_HIDDEN_CHUNK, (c + 1) * FFN_HIDDEN_CHUNK)
        g = _dot(h, wg_ref[:, cols])
        u = _dot(h, wu_ref[:, cols])
        a = (g * _sigmoid(g) * u).astype(BF16)
        acc = acc + _dot(a, wd_ref[cols, :])
    y = x + 0.5 * acc
    if final_norm:
        y = _rms(y, fgain_ref[...])
    o_ref[...] = y


def _ffn(x, gain, wg, wu, wd, fgain, *, final_norm):
    T, D = x.shape
    tm = FFN_TOKEN_TILE
    return pl.pallas_call(
        functools.partial(_ffn_kernel, final_norm=final_norm),
        grid=(T // tm,),
        in_specs=[
            pl.BlockSpec((tm, D), lambda i: (i, 0)),
            _resident((1, D)),
            _resident(wg.shape), _resident(wu.shape), _resident(wd.shape),
            _resident((1, D)),
        ],
        out_specs=pl.BlockSpec((tm, D), lambda i: (i, 0)),
        out_shape=jax.ShapeDtypeStruct((T, D), F32),
        compiler_params=_params("parallel"),
        name="ffn",
    )(x, gain, wg, wu, wd, fgain)


def _mix_in_kernel(x_ref, xp_ref, xn_ref, gain_ref, wqkv_ref, wz_ref, wba_ref, wpool_ref, wda_ref,
                   conv_ref, alog_ref, dtb_ref, cos_ref, sin_ref,
                   qkv_ref, z_ref, bg_ref, u_ref, daq_ref, dak_ref, dav_ref, *, tiles_per_seq):
    tm = x_ref.shape[0]
    HALO = V7X_SUBLANES
    half = DN_CONV // 2
    tile = pl.program_id(0) % tiles_per_seq
    gain = gain_ref[...]
    h = _rms(x_ref[...], gain).astype(BF16)

    h_halo = _rms(jnp.concatenate([xp_ref[...], xn_ref[...]], axis=0), gain).astype(BF16)
    wqkv = wqkv_ref[...]
    halo = _dot(h_halo, wqkv)
    prev = jnp.where(tile > 0, halo[:HALO], 0.0)
    nxt = jnp.where(tile < tiles_per_seq - 1, halo[HALO:], 0.0)
    win = jnp.concatenate([prev, _dot(h, wqkv), nxt], axis=0)
    cw = conv_ref[...]
    for cb in range(3 * DN_HEADS):
        cols = slice(cb * DN_HEAD_DIM, (cb + 1) * DN_HEAD_DIM)
        wc = win[:, cols]
        acc = jnp.zeros((tm, DN_HEAD_DIM), F32)
        for j in range(DN_CONV):
            off = HALO + j - half
            acc = acc + wc[off:off + tm, :] * cw[j:j + 1, cols]
        y = acc * _sigmoid(acc)
        if cb < 2 * DN_HEADS:
            y = y * lax.rsqrt(jnp.sum(y * y, axis=-1, keepdims=True) + L2_EPS)
        if cb < DN_HEADS:
            y = y * DN_HEAD_DIM ** -0.5
        qkv_ref[:, cols] = y

    z_ref[...] = _dot(h, wz_ref[...])

    ba = _dot(h, wba_ref[...])
    lane = lax.broadcasted_iota(jnp.int32, (1, V7X_LANES), 1)
    a_raw = ba + dtb_ref[...]
    softplus = jnp.maximum(a_raw, 0.0) + jnp.log(1.0 + jnp.exp(-jnp.abs(a_raw)))
    bg_ref[...] = jnp.where(lane < 2 * DN_HEADS, _sigmoid(ba), -jnp.exp(alog_ref[...]) * softplus)

    u_ref[...] = _dot(h, wpool_ref[...])
    da = _dot(h, wda_ref[...])
    cos = cos_ref[...]
    sin = sin_ref[...]
    for part, out_ref, scale in ((0, daq_ref, DA_HEAD_DIM ** -0.5), (1, dak_ref, 1.0)):
        for cb in range(DA_WIDTH // V7X_LANES):
            lo = part * DA_WIDTH + cb * V7X_LANES
            t = da[:, lo:lo + V7X_LANES]
            r = t * cos + pltpu.roll(t, V7X_LANES // 2, 1) * sin
            out_ref[:, cb * V7X_LANES:(cb + 1) * V7X_LANES] = r * scale
    dav_ref[...] = da[:, 2 * DA_WIDTH:]


def _lane_row(p):
    return jnp.pad(p.reshape(1, -1), ((0, 0), (2 * DN_HEADS, V7X_LANES - 4 * DN_HEADS)))


def _mix_in(x, gain, wqkv, wz, wba, wpool, wda, conv_w, a_log, dt_bias, cos, sin, seq_len):
    T, D = x.shape
    tm = MIX_TOKEN_TILE
    tiles_per_seq = seq_len // tm
    hb = tm // V7X_SUBLANES
    tok = lambda n: pl.BlockSpec((tm, n), lambda i: (i, 0))
    return pl.pallas_call(
        functools.partial(_mix_in_kernel, tiles_per_seq=tiles_per_seq),
        grid=(T // tm,),
        in_specs=[
            tok(D),
            pl.BlockSpec((V7X_SUBLANES, D), lambda i: (jnp.maximum(i * hb - 1, 0), 0)),
            pl.BlockSpec((V7X_SUBLANES, D), lambda i: (jnp.minimum((i + 1) * hb, T // V7X_SUBLANES - 1), 0)),
            _resident((1, D)),
            _resident(wqkv.shape), _resident(wz.shape), _resident(wba.shape),
            _resident(wpool.shape), _resident(wda.shape),
            _resident(conv_w.shape), _resident((1, V7X_LANES)), _resident((1, V7X_LANES)),
            pl.BlockSpec((tm, V7X_LANES), lambda i: (i % tiles_per_seq, 0)),
            pl.BlockSpec((tm, V7X_LANES), lambda i: (i % tiles_per_seq, 0)),
        ],
        out_specs=[tok(3 * DN_WIDTH), tok(DN_WIDTH), tok(V7X_LANES), tok(POOL_WIDTH),
                   tok(DA_WIDTH), tok(DA_WIDTH), tok(DA_WIDTH)],
        out_shape=[
            jax.ShapeDtypeStruct((T, 3 * DN_WIDTH), F32),
            jax.ShapeDtypeStruct((T, DN_WIDTH), F32),
            jax.ShapeDtypeStruct((T, V7X_LANES), F32),
            jax.ShapeDtypeStruct((T, POOL_WIDTH), F32),
            jax.ShapeDtypeStruct((T, DA_WIDTH), F32),
            jax.ShapeDtypeStruct((T, DA_WIDTH), F32),
            jax.ShapeDtypeStruct((T, DA_WIDTH), F32),
        ],
        compiler_params=_params("parallel"),
        name="mix_in",
    )(x, x, x, gain, wqkv, wz, wba, wpool, wda, conv_w, _lane_row(a_log), _lane_row(dt_bias), cos, sin)


def _unit_tri_inverse(mats, eye, ri, ci, n):
    blk = 2
    same = (ri // blk) == (ci // blk)
    ts = [eye - jnp.where(same, a, 0.0) for a in mats]
    while blk < n:
        joins = ((ri // (2 * blk)) == (ci // (2 * blk))) & ((ri // blk) != (ci // blk))
        ets = [_mm(jnp.where(joins, a, 0.0), t) for a, t in zip(mats, ts)]
        ts = [t - _mm(t, et) for t, et in zip(ts, ets)]
        blk *= 2
    return ts


def _deltanet_kernel(q_ref, k_ref, v_ref, bg_ref, o_ref,
                     beta_ref, gc_ref, u_ref, wq_ref, qk_ref, kd_ref, gl_ref, *, S, C):
    assert C == V7X_LANES
    head = pl.program_id(1)
    RC = ROW_CHUNK
    n_rc = S // RC

    lane = lax.broadcasted_iota(jnp.int32, (1, V7X_LANES), 1)
    rr = lax.broadcasted_iota(jnp.int32, (RC, RC), 0)
    rc = lax.broadcasted_iota(jnp.int32, (RC, RC), 1)
    same_chunk = (rr // C) == (rc // C)
    cum_mat = [(same_chunk & (rr >= rc)).astype(BF16), (same_chunk & (rr <= rc)).astype(BF16)]
    GRP = 4 * DN_HEADS

    def gate_body(i, carry):
        r0 = pl.multiple_of(i * RC, RC)
        bg = bg_ref[0, pl.ds(r0, RC), :]
        g_hi = bg.astype(BF16).astype(F32)
        g_mid = (bg - g_hi).astype(BF16).astype(F32)
        g_lo = (bg - g_hi) - g_mid
        parts = jnp.where(lane < GRP, g_hi,
                          jnp.where(lane < 2 * GRP, pltpu.roll(g_mid, GRP, 1),
                                    jnp.where(lane < 3 * GRP, pltpu.roll(g_lo, 2 * GRP, 1), 0.0))).astype(BF16)

        def pick(x, sel):
            col = jnp.sum(jnp.where(sel, x, 0.0), axis=-1, keepdims=True)
            return jnp.broadcast_to(col, (RC, DN_HEAD_DIM))

        for d in range(2):
            cum = _dot(cum_mat[d], parts)
            beta_ref[d, pl.ds(r0, RC), :] = pick(bg, lane == d * DN_HEADS + head)
            c = 2 * DN_HEADS + d * DN_HEADS + head
            gc_ref[d, pl.ds(r0, RC), :] = pick(cum, (lane == c) | (lane == c + GRP) | (lane == c + 2 * GRP))
        return carry

    lax.fori_loop(0, n_rc, gate_body, 0, unroll=2)

    ri = lax.broadcasted_iota(jnp.int32, (C, C), 0)
    ci = lax.broadcasted_iota(jnp.int32, (C, C), 1)
    eye = (ri == ci).astype(F32)
    n_chunks = S // C

    def chunk_rows(c_idx, n):
        return pl.ds(pl.multiple_of(c_idx * n, n), n)

    def prepare(chains):
        st = []
        for d, c_idx in chains:
            incl = (ri >= ci) if d == 0 else (ri <= ci)
            rows = chunk_rows(c_idx, C)
            q = q_ref[0, rows, :]
            k = k_ref[0, rows, :]
            beta = beta_ref[d, rows, :]
            gc = gc_ref[d, rows, :]
            last = C - 1 if d == 0 else 0
            st.append(dict(
                d=d, c=c_idx, rows=rows, q=q, k=k, gc=gc, kb=k * beta, vb=v_ref[0, rows, :] * beta,
                decay=jnp.where(incl, jnp.exp(gc - gc.T), 0.0),
                strict=(ri > ci) if d == 0 else (ri < ci),
                g_last=gc[last:last + 1, :]))
        kq = [_dot_nt(jnp.concatenate([s["kb"], s["q"]], axis=0).astype(BF16), s["k"].astype(BF16))
              for s in st]
        a = [jnp.where(s["strict"], m[:C] * s["decay"], 0.0) for s, m in zip(st, kq)]
        t = _unit_tri_inverse(a, eye, ri, ci, C)
        eg = [jnp.exp(s["gc"]) for s in st]
        sol = [_mm(ti, jnp.concatenate([s["vb"], s["kb"] * e], axis=1)) for ti, s, e in zip(t, st, eg)]
        for s, m, e, x in zip(st, kq, eg, sol):
            d, c_idx, rows = s["d"], s["c"], s["rows"]
            u_ref[d, rows, :] = x[:, :DN_HEAD_DIM]
            wq_ref[d, chunk_rows(c_idx, 2 * C), :] = jnp.concatenate(
                [x[:, DN_HEAD_DIM:], s["q"] * e], axis=0).astype(BF16)
            qk_ref[d, rows, :] = (m[C:] * s["decay"]).astype(BF16)
            kd_ref[d, rows, :] = (s["k"] * jnp.exp(s["g_last"] - s["gc"])).astype(BF16)
            gl_ref[d, chunk_rows(c_idx, V7X_SUBLANES), :] = jnp.broadcast_to(
                jnp.exp(s["g_last"]), (V7X_SUBLANES, DN_HEAD_DIM))

    def prepare_body(i, carry):
        prepare([(d, i * DN_PREP_CHUNKS + j) for j in range(DN_PREP_CHUNKS) for d in range(2)])
        return carry

    lax.fori_loop(0, n_chunks // DN_PREP_CHUNKS, prepare_body, 0)

    def chunk_body(c, states, *, accumulate):
        idx = (c, n_chunks - 1 - c)
        s16 = [s.astype(BF16) for s in states]
        ws = [_dot(wq_ref[d, chunk_rows(idx[d], 2 * C), :], s16[d]) for d in range(2)]
        v_new = [(u_ref[d, chunk_rows(idx[d], C), :] - ws[d][:C]).astype(BF16) for d in range(2)]
        for d in range(2):
            o = ws[d][C:] + _dot(qk_ref[d, chunk_rows(idx[d], C), :], v_new[d])
            rows = chunk_rows(idx[d], C)
            if accumulate:
                o_ref[0, rows, :] += o
            else:
                o_ref[0, rows, :] = o
        return tuple(
            states[d] * gl_ref[d, chunk_rows(idx[d], V7X_SUBLANES), :][0:1, :]
            + _dot_tn(kd_ref[d, chunk_rows(idx[d], C), :], v_new[d]) for d in range(2))

    assert n_chunks % 2 == 0
    zero_state = jnp.zeros((DN_HEAD_DIM, DN_HEAD_DIM), F32)
    states = lax.fori_loop(0, n_chunks // 2, functools.partial(chunk_body, accumulate=False),
                           (zero_state, zero_state))
    lax.fori_loop(n_chunks // 2, n_chunks, functools.partial(chunk_body, accumulate=True), states)


def _deltanet(qkv, bg):
    B, S, _ = qkv.shape
    C = DN_CHUNK
    hd = DN_HEAD_DIM
    seq = lambda off: pl.BlockSpec((1, S, hd), lambda b, h: (b, 0, off + h))
    return pl.pallas_call(
        functools.partial(_deltanet_kernel, S=S, C=C),
        grid=(B, DN_HEADS),
        in_specs=[
            seq(0), seq(DN_HEADS), seq(2 * DN_HEADS),
            pl.BlockSpec((1, S, V7X_LANES), lambda b, h: (b, 0, 0)),
        ],
        out_specs=pl.BlockSpec((1, S, hd), lambda b, h: (b, 0, h)),
        out_shape=jax.ShapeDtypeStruct((B, S, DN_WIDTH), F32),
        scratch_shapes=[
            pltpu.VMEM((2, S, hd), F32), pltpu.VMEM((2, S, hd), F32),
            pltpu.VMEM((2, S, hd), F32),
            pltpu.VMEM((2, 2 * S, hd), BF16),
            pltpu.VMEM((2, S, C), BF16),
            pltpu.VMEM((2, S, hd), BF16),
            pltpu.VMEM((2, (S // C) * V7X_SUBLANES, hd), F32),
        ],
        compiler_params=_params("parallel", "parallel"),
        name="deltanet",
    )(qkv, qkv, qkv, bg)


def _pool_kernel(u_ref, w_ref, sc_ref, o_ref, pad_ref, *, S):
    grp = pl.program_id(1)
    RC = ROW_CHUNK
    HALO = 2 * V7X_SUBLANES
    gd = POOL_GROUP_DIM
    pad_ref[0:HALO, :] = jnp.zeros((HALO, gd), F32)
    pad_ref[HALO + S:2 * HALO + S, :] = jnp.zeros((HALO, gd), F32)
    pad_ref[HALO:HALO + S, :] = u_ref[0]
    w16 = w_ref[0].astype(BF16)
    scale = sc_ref[...]

    for gi, win in enumerate(POOL_WINDOWS):
        lo_off = -(win // 2)
        hi_off = win - win // 2

        @pl.when(grp == gi)
        def _():
            def body(i, carry):
                r0 = pl.multiple_of(i * RC, RC)
                blk = pad_ref[pl.ds(r0, RC + 2 * HALO), :]
                acc = jnp.zeros((RC, gd), F32)
                for off in range(lo_off, hi_off):
                    acc = acc + blk[HALO + off:HALO + off + RC, :]
                pos = r0 + lax.broadcasted_iota(jnp.int32, (RC, 1), 0)
                cnt = jnp.minimum(pos + hi_off, S) - jnp.maximum(pos + lo_off, 0)
                pooled = acc / cnt.astype(F32) - blk[HALO:HALO + RC, :]
                mixed = _dot(pooled.astype(BF16), w16) * scale
                o_ref[0, pl.ds(r0, RC), :] = mixed.astype(o_ref.dtype)
                return carry

            lax.fori_loop(0, S // RC, body, 0)


def _pool(u, pool_w, pool_scale):
    B, S, _ = u.shape
    gd = POOL_GROUP_DIM
    return pl.pallas_call(
        functools.partial(_pool_kernel, S=S),
        grid=(B, len(POOL_WINDOWS)),
        in_specs=[
            pl.BlockSpec((1, S, gd), lambda b, g: (b, 0, g)),
            pl.BlockSpec((1, gd, gd), lambda b, g: (g, 0, 0)),
            pl.BlockSpec((1, gd), lambda b, g: (0, g)),
        ],
        out_specs=pl.BlockSpec((1, S, gd), lambda b, g: (b, 0, g)),
        out_shape=jax.ShapeDtypeStruct((B, S, POOL_WIDTH), BF16),
        scratch_shapes=[pltpu.VMEM((S + 4 * V7X_SUBLANES, gd), F32)],
        compiler_params=_params("parallel", "parallel"),
        name="pool",
    )(u, pool_w, pool_scale)


def _dilated_kernel(q_ref, k_ref, v_ref, o_ref,
                    qd_ref, kd_ref, vd_ref, od_ref, ld_ref, on_ref, ln_ref, num_ref, den_ref, mx_ref, *, S):
    grp = pl.program_id(2)
    QB = DA_QBLOCK
    W = V7X_LANES
    n_it = S // QB
    lane = lax.broadcasted_iota(jnp.int32, (1, W), 1)
    pair_mask = [((lane // (DA_HEAD_DIM // 2)) % 2 == e) for e in range(2)]
    first_head = lane < DA_HEAD_DIM
    qi2 = lax.broadcasted_iota(jnp.int32, (2 * QB, 2 * QB), 0) % QB
    kj2 = lax.broadcasted_iota(jnp.int32, (2 * QB, 2 * QB), 1)

    for gi, (window, dil) in enumerate(DA_CONFIGS):
        radius = window // (2 * dil)
        assert radius == QB // 2
        L = S // dil
        LP = L + QB
        blocks_per_res = L // QB
        band2 = jnp.abs(kj2 - radius - qi2) <= radius

        @pl.when(grp == gi)
        def _():
            zpad = jnp.zeros((radius, W), BF16)
            for r in range(dil):
                src = pl.ds(r, L, stride=dil) if dil > 1 else pl.ds(0, L)
                qd_ref[r * L:(r + 1) * L, :] = q_ref[0, src, :].astype(BF16)
                base = r * LP
                kd_ref[base:base + radius, :] = zpad
                vd_ref[base:base + radius, :] = zpad
                kd_ref[base + radius:base + radius + L, :] = k_ref[0, src, :].astype(BF16)
                vd_ref[base + radius:base + radius + L, :] = v_ref[0, src, :].astype(BF16)
                kd_ref[base + radius + L:base + LP, :] = zpad
                vd_ref[base + radius + L:base + LP, :] = zpad

            def body(step, carry):
                blocks = []
                for j in range(DA_LOCKSTEP):
                    it = step * DA_LOCKSTEP + j
                    res = it // blocks_per_res
                    m0 = (it % blocks_per_res) * QB
                    q0 = pl.multiple_of(it * QB, QB)
                    k0 = pl.multiple_of(it * QB + res * QB, QB)
                    kpos = kj2 + (m0 - radius)
                    qp = qd_ref[pl.ds(q0, QB), :]
                    zero = jnp.zeros_like(qp)
                    blocks.append(dict(
                        q0=q0, valid=band2 & (kpos >= 0) & (kpos < L),
                        q=jnp.concatenate([jnp.where(pair_mask[e], qp, zero) for e in range(2)], axis=0),
                        k=kd_ref[pl.ds(k0, 2 * QB), :], v=vd_ref[pl.ds(k0, 2 * QB), :]))
                scores = [jnp.where(b["valid"], _dot_nt(b["q"], b["k"]), MASK_VALUE) for b in blocks]
                mxs = [jnp.max(s, axis=-1, keepdims=True) for s in scores]
                probs = [jnp.exp(s - m) for s, m in zip(scores, mxs)]
                dens = [jnp.sum(p, axis=-1, keepdims=True) for p in probs]
                outs = [_dot(p.astype(BF16), b["v"]) / d for p, b, d in zip(probs, blocks, dens)]
                for b, o, m, d in zip(blocks, outs, mxs, dens):
                    lse = m + jnp.log(d)
                    od_ref[pl.ds(b["q0"], QB), :] = jnp.where(first_head, o[:QB], o[QB:])
                    ld_ref[pl.ds(b["q0"], QB), :] = jnp.where(first_head, lse[:QB], lse[QB:])
                return carry

            lax.fori_loop(0, n_it // DA_LOCKSTEP, body, 0)

            for r in range(dil):
                dst = pl.ds(r, L, stride=dil) if dil > 1 else pl.ds(0, L)
                on_ref[dst, :] = od_ref[r * L:(r + 1) * L, :]
                ln_ref[dst, :] = ld_ref[r * L:(r + 1) * L, :]

    @pl.when(grp == 0)
    def _():
        num_ref[...] = on_ref[...]
        den_ref[...] = jnp.ones_like(den_ref)
        mx_ref[...] = ln_ref[...]

    @pl.when(grp > 0)
    def _():
        m_old = mx_ref[...]
        lse = ln_ref[...]
        m_new = jnp.maximum(m_old, lse)
        a_old = jnp.exp(m_old - m_new)
        a_new = jnp.exp(lse - m_new)
        num_ref[...] = num_ref[...] * a_old + on_ref[...] * a_new
        den_ref[...] = den_ref[...] * a_old + a_new
        mx_ref[...] = m_new

    @pl.when(grp == DA_NGROUPS - 1)
    def _():
        o_ref[0] = (num_ref[...] / den_ref[...]).astype(o_ref.dtype)


def _dilated(daq, dak, dav):
    B, S, _ = daq.shape
    W = V7X_LANES
    pairs = DA_GROUP_WIDTH // W
    max_dil = max(d for _, d in DA_CONFIGS)
    grp_spec = pl.BlockSpec((1, S, W), lambda b, p, g: (b, 0, g * pairs + p))
    full = lambda dt: pltpu.VMEM((S, W), dt)
    return pl.pallas_call(
        functools.partial(_dilated_kernel, S=S),
        grid=(B, pairs, DA_NGROUPS),
        in_specs=[grp_spec, grp_spec, grp_spec],
        out_specs=pl.BlockSpec((1, S, W), lambda b, p, g: (b, 0, p)),
        out_shape=jax.ShapeDtypeStruct((B, S, DA_GROUP_WIDTH), BF16),
        scratch_shapes=[
            full(BF16),
            pltpu.VMEM((S + max_dil * DA_QBLOCK, W), BF16),
            pltpu.VMEM((S + max_dil * DA_QBLOCK, W), BF16),
            full(F32), full(F32), full(F32), full(F32), full(F32), full(F32), full(F32),
        ],
        compiler_params=_params("parallel", "parallel", "arbitrary"),
        name="dilated",
    )(daq, dak, dav)


def _mix_out_kernel(x_ref, gain_ref, oa_ref, z_ref, onorm_ref, yb_ref, yc_ref, wg_ref, bg_ref, wa_ref, wb_ref,
                    wc_ref, wo_ref, o_ref):
    x = x_ref[...]
    D = x.shape[-1]
    h = _rms(x, gain_ref[...]).astype(BF16)
    o = oa_ref[...]
    z = z_ref[...]
    heads = []
    for hd in range(DN_HEADS):
        cols = slice(hd * DN_HEAD_DIM, (hd + 1) * DN_HEAD_DIM)
        zh = z[:, cols]
        heads.append((_rms(o[:, cols], onorm_ref[...]) * (zh * _sigmoid(zh))).astype(BF16))
    ya = jnp.concatenate(heads, axis=1)
    merged = jnp.zeros(x.shape, F32)
    for i, (y, w_ref) in enumerate(((ya, wa_ref), (yb_ref[...], wb_ref), (yc_ref[...], wc_ref))):
        gate = _sigmoid(_dot(h, wg_ref[:, i * D:(i + 1) * D]) + bg_ref[:, i * D:(i + 1) * D])
        merged = merged + gate * _dot(y, w_ref[...])
    o_ref[...] = x + _dot(merged.astype(BF16), wo_ref[...])


def _mix_out(x, gain, oa, z, onorm, yb, yc, wg, bg, wa, wb, wc, wo):
    T, D = x.shape
    tm = MIX_TOKEN_TILE
    tok = lambda n: pl.BlockSpec((tm, n), lambda i: (i, 0))
    return pl.pallas_call(
        _mix_out_kernel,
        grid=(T // tm,),
        in_specs=[
            tok(D), _resident((1, D)),
            tok(oa.shape[1]), tok(z.shape[1]), _resident(onorm.shape), tok(yb.shape[1]), tok(yc.shape[1]),
            _resident(wg.shape), _resident(bg.shape), _resident(wa.shape), _resident(wb.shape),
            _resident(wc.shape), _resident(wo.shape),
        ],
        out_specs=tok(D),
        out_shape=jax.ShapeDtypeStruct((T, D), F32),
        compiler_params=_params("parallel"),
        name="mix_out",
    )(x, gain, oa, z, onorm, yb, yc, wg, bg, wa, wb, wc, wo)


def _pair_layout_columns():
    half = DA_HEAD_DIM // 2
    order = []
    for g in range(DA_NGROUPS):
        for p in range(DA_HEADS_PER_GROUP // 2):
            for part in range(2):
                for e in range(2):
                    head = 2 * p + e
                    start = g * DA_GROUP_WIDTH + head * DA_HEAD_DIM + part * half
                    order.extend(range(start, start + half))
    return jnp.asarray(order, dtype=jnp.int32)


def _rope_tables(seq_len):
    half = DA_HEAD_DIM // 2
    inv_freq = ROPE_THETA ** (-jnp.arange(half, dtype=F32) / half)
    ang = jnp.arange(seq_len).astype(F32)[:, None] * inv_freq[None, :]
    cos = jnp.tile(jnp.cos(ang), (1, V7X_LANES // half))
    sin = jnp.sin(ang)
    sin = jnp.concatenate([-sin, -sin, sin, sin], axis=1)
    return cos, sin


def kernel(x, ffn1_norm, ffn1_w_gate, ffn1_w_up, ffn1_w_down, mix_norm, w_in, dn_conv, dn_a_log, dn_dt_bias, dn_out_norm, pool_w, pool_scale, w_proj_a, w_proj_b, w_proj_c, w_gate, b_gate, w_out, ffn2_norm, ffn2_w_gate, ffn2_w_up, ffn2_w_down, final_norm):
    B, S, D = x.shape
    depth = w_in.shape[0]
    T = B * S
    off_z = 3 * DN_WIDTH
    off_beta = off_z + DN_WIDTH
    off_pool = off_beta + 4 * DN_HEADS
    off_da = off_pool + POOL_WIDTH
    pair_cols = _pair_layout_columns()
    cos, sin = _rope_tables(S)
    fgain = final_norm.reshape(1, D)

    xt = x.reshape(T, D)
    for l in range(depth):
        xt = _ffn(xt, ffn1_norm[l].reshape(1, D), ffn1_w_gate[l].astype(BF16), ffn1_w_up[l].astype(BF16),
                  ffn1_w_down[l].astype(BF16), fgain, final_norm=False)

        w = w_in[l]
        w_da = w[:, off_da:]
        w_da = jnp.concatenate([w_da[:, :DA_WIDTH][:, pair_cols],
                                w_da[:, DA_WIDTH:2 * DA_WIDTH][:, pair_cols],
                                w_da[:, 2 * DA_WIDTH:]], axis=1)
        w_ba = jnp.pad(w[:, off_beta:off_pool], ((0, 0), (0, V7X_LANES - 4 * DN_HEADS)))
        qkv, z, bg, u, daq, dak, dav = _mix_in(
            xt, mix_norm[l].reshape(1, D), w[:, :off_z].astype(BF16), w[:, off_z:off_beta].astype(BF16),
            w_ba.astype(BF16), w[:, off_pool:off_da].astype(BF16), w_da.astype(BF16), dn_conv[l],
            dn_a_log[l], dn_dt_bias[l], cos, sin, S)

        o_a = _deltanet(qkv.reshape(B, S, -1), bg.reshape(B, S, -1))
        y_b = _pool(u.reshape(B, S, -1), pool_w[l], pool_scale[l].reshape(1, POOL_WIDTH))
        y_c = _dilated(daq.reshape(B, S, -1), dak.reshape(B, S, -1), dav.reshape(B, S, -1))

        xt = _mix_out(xt, mix_norm[l].reshape(1, D), o_a.reshape(T, -1), z,
                      dn_out_norm[l].reshape(1, DN_HEAD_DIM), y_b.reshape(T, -1),
                      y_c.reshape(T, -1), w_gate[l].astype(BF16), b_gate[l].reshape(1, -1),
                      w_proj_a[l].astype(BF16), w_proj_b[l].astype(BF16), w_proj_c[l].astype(BF16),
                      w_out[l].astype(BF16))

        xt = _ffn(xt, ffn2_norm[l].reshape(1, D), ffn2_w_gate[l].astype(BF16), ffn2_w_up[l].astype(BF16),
                  ffn2_w_down[l].astype(BF16), fgain, final_norm=(l == depth - 1))
    return xt.reshape(B, S, D)
```

```python
import functools

import jax
import jax.numpy as jnp
from jax import lax
from jax.experimental import pallas as pl
from jax.experimental.pallas import tpu as pltpu

F32 = jnp.float32
BF16 = jnp.bfloat16

RMS_EPS = 1e-6
L2_EPS = 1e-6

DN_HEADS = 4
DN_HEAD_DIM = 128
DN_WIDTH = DN_HEADS * DN_HEAD_DIM
DN_CONV = 5
DN_CHUNK = 128
DN_PREP_CHUNKS = 8
DN_HEADS_PER_STEP = 2

POOL_WINDOWS = (2, 4, 8, 16)
POOL_GROUP_DIM = 128
POOL_WIDTH = len(POOL_WINDOWS) * POOL_GROUP_DIM

DA_CONFIGS = ((128, 1), (512, 4), (2048, 16))
DA_NGROUPS = len(DA_CONFIGS)
DA_HEADS_PER_GROUP = 4
DA_HEAD_DIM = 64
DA_GROUP_WIDTH = DA_HEADS_PER_GROUP * DA_HEAD_DIM
DA_WIDTH = DA_NGROUPS * DA_GROUP_WIDTH
DA_QBLOCK = 128
DA_LOCKSTEP = 4
ROPE_THETA = 10000.0
MASK_VALUE = -1e30

N_BRANCHES = 3

V7X_LANES = 128
V7X_SUBLANES = 8
V7X_VMEM_LIMIT_BYTES = 56 * 1024 * 1024

V7X_MXU_WIDTH = 256

FFN_TOKEN_TILE = 512
MIX_TOKEN_TILE = 512
ROW_CHUNK = 256


def _params(*semantics):
    return pltpu.CompilerParams(dimension_semantics=semantics,
                                vmem_limit_bytes=V7X_VMEM_LIMIT_BYTES)


def _resident(shape):
    nd = len(shape)
    return pl.BlockSpec(shape, lambda *_: (0,) * nd, pipeline_mode=pl.Buffered(1))


def _rms(x, gain):
    return x * lax.rsqrt(jnp.mean(x * x, axis=-1, keepdims=True) + RMS_EPS) * gain


def _sigmoid(x):
    return 1.0 / (1.0 + jnp.exp(-x))


def _dot(a, b):
    return jnp.dot(a, b, preferred_element_type=F32)


def _dot_nt(a, b):
    return lax.dot_general(a, b, (((1,), (1,)), ((), ())), preferred_element_type=F32)


def _dot_tn(a, b):
    return lax.dot_general(a, b, (((0,), (0,)), ((), ())), preferred_element_type=F32)


def _mm(a, b):
    return _dot(a.astype(BF16), b.astype(BF16))


def _ffn_kernel(x_ref, gain_ref, wg_ref, wu_ref, wd_ref, fgain_ref, o_ref, *, final_norm):
    x = x_ref[...]
    h = _rms(x, gain_ref[...]).astype(BF16)
    F = wg_ref.shape[1]
    acc = jnp.zeros(x.shape, F32)
    split = (F // (2 * V7X_MXU_WIDTH)) * V7X_MXU_WIDTH
    for cols in (slice(0, split), slice(split, F)):
        g = _dot(h, wg_ref[:, cols])
        u = _dot(h, wu_ref[:, cols])
        a = (g * _sigmoid(g) * u).astype(BF16)
        acc = acc + _dot(a, wd_ref[cols, :])
    y = x + 0.5 * acc
    if final_norm:
        y = _rms(y, fgain_ref[...])
    o_ref[...] = y


def _ffn(x, gain, wg, wu, wd, fgain, *, final_norm):
    T, D = x.shape
    tm = FFN_TOKEN_TILE
    return pl.pallas_call(
        functools.partial(_ffn_kernel, final_norm=final_norm),
        grid=(T // tm,),
        in_specs=[
            pl.BlockSpec((tm, D), lambda i: (i, 0)),
            _resident((1, D)),
            _resident(wg.shape), _resident(wu.shape), _resident(wd.shape),
            _resident((1, D)),
        ],
        out_specs=pl.BlockSpec((tm, D), lambda i: (i, 0)),
        out_shape=jax.ShapeDtypeStruct((T, D), F32),
        compiler_params=_params("parallel"),
        name="ffn",
    )(x, gain, wg, wu, wd, fgain)


def _mix_in_kernel(x_ref, xp_ref, xn_ref, gain_ref, wqkv_ref, wz_ref, wba_ref, wpool_ref, wda_ref,
                   conv_ref, alog_ref, dtb_ref, cos_ref, sin_ref,
                   qkv_ref, z_ref, bg_ref, u_ref, daq_ref, dak_ref, dav_ref, *, tiles_per_seq):
    tm = x_ref.shape[0]
    HALO = V7X_SUBLANES
    half = DN_CONV // 2
    tile = pl.program_id(0) % tiles_per_seq
    gain = gain_ref[...]
    h = _rms(x_ref[...], gain).astype(BF16)

    h_halo = _rms(jnp.concatenate([xp_ref[...], xn_ref[...]], axis=0), gain).astype(BF16)
    wqkv = wqkv_ref[...]
    halo = _dot(h_halo, wqkv)
    prev = jnp.where(tile > 0, halo[:HALO], 0.0)
    nxt = jnp.where(tile < tiles_per_seq - 1, halo[HALO:], 0.0)
    win = jnp.concatenate([prev, _dot(h, wqkv), nxt], axis=0)
    cw = conv_ref[...]
    for cb in range(3 * DN_HEADS):
        cols = slice(cb * DN_HEAD_DIM, (cb + 1) * DN_HEAD_DIM)
        wc = win[:, cols]
        acc = jnp.zeros((tm, DN_HEAD_DIM), F32)
        for j in range(DN_CONV):
            off = HALO + j - half
            acc = acc + wc[off:off + tm, :] * cw[j:j + 1, cols]
        y = acc * _sigmoid(acc)
        if cb < 2 * DN_HEADS:
            y = y * lax.rsqrt(jnp.sum(y * y, axis=-1, keepdims=True) + L2_EPS)
        if cb < DN_HEADS:
            y = y * DN_HEAD_DIM ** -0.5
        qkv_ref[:, cols] = y

    z_ref[...] = _dot(h, wz_ref[...])

    ba = _dot(h, wba_ref[...])
    lane = lax.broadcasted_iota(jnp.int32, (1, V7X_LANES), 1)
    a_raw = ba + dtb_ref[...]
    softplus = jnp.maximum(a_raw, 0.0) + jnp.log(1.0 + jnp.exp(-jnp.abs(a_raw)))
    bg_ref[...] = jnp.where(lane < 2 * DN_HEADS, _sigmoid(ba), -jnp.exp(alog_ref[...]) * softplus)

    u_ref[...] = _dot(h, wpool_ref[...])
    da = _dot(h, wda_ref[...])
    cos = cos_ref[...]
    sin = sin_ref[...]
    for part, out_ref, scale in ((0, daq_ref, DA_HEAD_DIM ** -0.5), (1, dak_ref, 1.0)):
        for cb in range(DA_WIDTH // V7X_LANES):
            lo = part * DA_WIDTH + cb * V7X_LANES
            t = da[:, lo:lo + V7X_LANES]
            r = t * cos + pltpu.roll(t, V7X_LANES // 2, 1) * sin
            out_ref[:, cb * V7X_LANES:(cb + 1) * V7X_LANES] = r * scale
    dav_ref[...] = da[:, 2 * DA_WIDTH:]


def _lane_row(p):
    return jnp.pad(p.reshape(1, -1), ((0, 0), (2 * DN_HEADS, V7X_LANES - 4 * DN_HEADS)))


def _mix_in(x, gain, wqkv, wz, wba, wpool, wda, conv_w, a_log, dt_bias, cos, sin, seq_len):
    T, D = x.shape
    tm = MIX_TOKEN_TILE
    tiles_per_seq = seq_len // tm
    hb = tm // V7X_SUBLANES
    tok = lambda n: pl.BlockSpec((tm, n), lambda i: (i, 0))
    return pl.pallas_call(
        functools.partial(_mix_in_kernel, tiles_per_seq=tiles_per_seq),
        grid=(T // tm,),
        in_specs=[
            tok(D),
            pl.BlockSpec((V7X_SUBLANES, D), lambda i: (jnp.maximum(i * hb - 1, 0), 0)),
            pl.BlockSpec((V7X_SUBLANES, D), lambda i: (jnp.minimum((i + 1) * hb, T // V7X_SUBLANES - 1), 0)),
            _resident((1, D)),
            _resident(wqkv.shape), _resident(wz.shape), _resident(wba.shape),
            _resident(wpool.shape), _resident(wda.shape),
            _resident(conv_w.shape), _resident((1, V7X_LANES)), _resident((1, V7X_LANES)),
            pl.BlockSpec((tm, V7X_LANES), lambda i: (i % tiles_per_seq, 0)),
            pl.BlockSpec((tm, V7X_LANES), lambda i: (i % tiles_per_seq, 0)),
        ],
        out_specs=[tok(3 * DN_WIDTH), tok(DN_WIDTH), tok(V7X_LANES), tok(POOL_WIDTH),
                   tok(DA_WIDTH), tok(DA_WIDTH), tok(DA_WIDTH)],
        out_shape=[
            jax.ShapeDtypeStruct((T, 3 * DN_WIDTH), F32),
            jax.ShapeDtypeStruct((T, DN_WIDTH), F32),
            jax.ShapeDtypeStruct((T, V7X_LANES), F32),
            jax.ShapeDtypeStruct((T, POOL_WIDTH), F32),
            jax.ShapeDtypeStruct((T, DA_WIDTH), F32),
            jax.ShapeDtypeStruct((T, DA_WIDTH), F32),
            jax.ShapeDtypeStruct((T, DA_WIDTH), F32),
        ],
        compiler_params=_params("parallel"),
        name="mix_in",
    )(x, x, x, gain, wqkv, wz, wba, wpool, wda, conv_w, _lane_row(a_log), _lane_row(dt_bias), cos, sin)


def _unit_tri_inverse(mats, eye, ri, ci, n):
    blk = 2
    same = (ri // blk) == (ci // blk)
    ts = [eye - jnp.where(same, a, 0.0) for a in mats]
    while blk < n:
        joins = ((ri // (2 * blk)) == (ci // (2 * blk))) & ((ri // blk) != (ci // blk))
        ets = [_mm(jnp.where(joins, a, 0.0), t) for a, t in zip(mats, ts)]
        ts = [t - _mm(t, et) for t, et in zip(ts, ets)]
        blk *= 2
    return ts


def _deltanet_kernel(q_ref, k_ref, v_ref, bg_ref, o_ref,
                     beta_ref, gc_ref, u_ref, wq_ref, qk_ref, kd_ref, gl_ref, *, S, C):
    assert C == V7X_LANES
    RC = ROW_CHUNK
    n_rc = S // RC
    NH = DN_HEADS_PER_STEP

    lane = lax.broadcasted_iota(jnp.int32, (1, V7X_LANES), 1)
    rr = lax.broadcasted_iota(jnp.int32, (RC, RC), 0)
    rc = lax.broadcasted_iota(jnp.int32, (RC, RC), 1)
    same_chunk = (rr // C) == (rc // C)
    cum_mat = [(same_chunk & (rr >= rc)).astype(BF16), (same_chunk & (rr <= rc)).astype(BF16)]
    GRP = 4 * DN_HEADS

    def gate_body(i, carry):
        r0 = pl.multiple_of(i * RC, RC)
        bg = bg_ref[0, pl.ds(r0, RC), :]
        g_hi = bg.astype(BF16).astype(F32)
        g_mid = (bg - g_hi).astype(BF16).astype(F32)
        g_lo = (bg - g_hi) - g_mid
        parts = jnp.where(lane < GRP, g_hi,
                          jnp.where(lane < 2 * GRP, pltpu.roll(g_mid, GRP, 1),
                                    jnp.where(lane < 3 * GRP, pltpu.roll(g_lo, 2 * GRP, 1), 0.0))).astype(BF16)

        def pick(x, sel):
            col = jnp.sum(jnp.where(sel, x, 0.0), axis=-1, keepdims=True)
            return jnp.broadcast_to(col, (RC, DN_HEAD_DIM))

        for d in range(2):
            cum = _dot(cum_mat[d], parts)
            for hh in range(NH):
                head = pl.program_id(1) * NH + hh
                beta_ref[2 * hh + d, pl.ds(r0, RC), :] = pick(bg, lane == d * DN_HEADS + head)
                c = 2 * DN_HEADS + d * DN_HEADS + head
                gc_ref[2 * hh + d, pl.ds(r0, RC), :] = pick(
                    cum, (lane == c) | (lane == c + GRP) | (lane == c + 2 * GRP))
        return carry

    lax.fori_loop(0, n_rc, gate_body, 0, unroll=2)

    ri = lax.broadcasted_iota(jnp.int32, (C, C), 0)
    ci = lax.broadcasted_iota(jnp.int32, (C, C), 1)
    eye = (ri == ci).astype(F32)
    incl = [ri >= ci, ri <= ci]
    strict = [ri > ci, ri < ci]
    n_chunks = S // C

    def chunk_rows(c_idx, n):
        return pl.ds(pl.multiple_of(c_idx * n, n), n)

    def prepare(hh, chunks):
        cols = slice(hh * DN_HEAD_DIM, (hh + 1) * DN_HEAD_DIM)
        st = []
        for c_idx in chunks:
            rows = chunk_rows(c_idx, C)
            k = k_ref[0, rows, cols]
            v = v_ref[0, rows, cols]
            beta = [beta_ref[2 * hh + d, rows, :] for d in range(2)]
            gc = [gc_ref[2 * hh + d, rows, :] for d in range(2)]
            st.append(dict(
                c=c_idx, rows=rows, q=q_ref[0, rows, cols], k=k, gc=gc,
                kb=[k * b for b in beta], vb=[v * b for b in beta],
                decay=[jnp.where(incl[d], jnp.exp(gc[d] - gc[d].T), 0.0) for d in range(2)],
                g_last=[gc[0][C - 1:C, :], gc[1][0:1, :]]))
        kq = [_dot_nt(jnp.concatenate(s["kb"] + [s["q"]], axis=0).astype(BF16), s["k"].astype(BF16)) for s in st]
        a = [jnp.where(strict[d], m[d * C:(d + 1) * C] * s["decay"][d], 0.0)
             for s, m in zip(st, kq) for d in range(2)]
        t = _unit_tri_inverse(a, eye, ri, ci, C)
        eg = [jnp.exp(s["gc"][d]) for s in st for d in range(2)]
        sol = [_mm(t[2 * n + d], jnp.concatenate([s["vb"][d], s["kb"][d] * eg[2 * n + d]], axis=1))
               for n, s in enumerate(st) for d in range(2)]
        for n, (s, m) in enumerate(zip(st, kq)):
            for d in range(2):
                x, e = sol[2 * n + d], eg[2 * n + d]
                i, c_idx, rows = 2 * hh + d, s["c"], s["rows"]
                u_ref[i, rows, :] = x[:, :DN_HEAD_DIM]
                wq_ref[i, chunk_rows(c_idx, 2 * C), :] = jnp.concatenate(
                    [x[:, DN_HEAD_DIM:], s["q"] * e], axis=0).astype(BF16)
                qk_ref[i, rows, :] = (m[2 * C:] * s["decay"][d]).astype(BF16)
                kd_ref[i, rows, :] = (s["k"] * jnp.exp(s["g_last"][d] - s["gc"][d])).astype(BF16)
                gl_ref[i, chunk_rows(c_idx, V7X_SUBLANES), :] = jnp.broadcast_to(
                    jnp.exp(s["g_last"][d]), (V7X_SUBLANES, DN_HEAD_DIM))

    for hh in range(NH):
        def prepare_body(i, carry, hh=hh):
            prepare(hh, [i * DN_PREP_CHUNKS + j for j in range(DN_PREP_CHUNKS)])
            return carry

        lax.fori_loop(0, n_chunks // DN_PREP_CHUNKS, prepare_body, 0)

    chains = [(hh, d) for hh in range(NH) for d in range(2)]

    def chunk_body(c, states, *, accumulate):
        idx = (c, n_chunks - 1 - c)
        s16 = [s.astype(BF16) for s in states]
        ws = [_dot(wq_ref[2 * hh + d, chunk_rows(idx[d], 2 * C), :], s16[n])
              for n, (hh, d) in enumerate(chains)]
        v_new = [(u_ref[2 * hh + d, chunk_rows(idx[d], C), :] - ws[n][:C]).astype(BF16)
                 for n, (hh, d) in enumerate(chains)]
        for n, (hh, d) in enumerate(chains):
            o = ws[n][C:] + _dot(qk_ref[2 * hh + d, chunk_rows(idx[d], C), :], v_new[n])
            rows = chunk_rows(idx[d], C)
            cols = slice(hh * DN_HEAD_DIM, (hh + 1) * DN_HEAD_DIM)
            if accumulate:
                o_ref[0, rows, cols] += o
            else:
                o_ref[0, rows, cols] = o
        return tuple(
            states[n] * gl_ref[2 * hh + d, chunk_rows(idx[d], V7X_SUBLANES), :][0:1, :]
            + _dot_tn(kd_ref[2 * hh + d, chunk_rows(idx[d], C), :], v_new[n])
            for n, (hh, d) in enumerate(chains))

    assert n_chunks % 2 == 0
    zero_state = jnp.zeros((DN_HEAD_DIM, DN_HEAD_DIM), F32)
    states = lax.fori_loop(0, n_chunks // 2, functools.partial(chunk_body, accumulate=False),
                           (zero_state,) * len(chains))
    lax.fori_loop(n_chunks // 2, n_chunks, functools.partial(chunk_body, accumulate=True), states)


def _deltanet(qkv, bg):
    B, S, _ = qkv.shape
    C = DN_CHUNK
    hd = DN_HEAD_DIM
    NH = DN_HEADS_PER_STEP
    steps = DN_HEADS // NH
    seq = lambda off: pl.BlockSpec((1, S, NH * hd), lambda b, h: (b, 0, off + h))
    lanes = NH * 2
    return pl.pallas_call(
        functools.partial(_deltanet_kernel, S=S, C=C),
        grid=(B, steps),
        in_specs=[
            seq(0), seq(steps), seq(2 * steps),
            pl.BlockSpec((1, S, V7X_LANES), lambda b, h: (b, 0, 0)),
        ],
        out_specs=pl.BlockSpec((1, S, NH * hd), lambda b, h: (b, 0, h)),
        out_shape=jax.ShapeDtypeStruct((B, S, DN_WIDTH), F32),
        scratch_shapes=[
            pltpu.VMEM((lanes, S, hd), F32), pltpu.VMEM((lanes, S, hd), F32),
            pltpu.VMEM((lanes, S, hd), F32),
            pltpu.VMEM((lanes, 2 * S, hd), BF16),
            pltpu.VMEM((lanes, S, C), BF16),
            pltpu.VMEM((lanes, S, hd), BF16),
            pltpu.VMEM((lanes, (S // C) * V7X_SUBLANES, hd), F32),
        ],
        compiler_params=_params("parallel", "parallel"),
        name="deltanet",
    )(qkv, qkv, qkv, bg)


POOL_HALO = 2 * V7X_SUBLANES
POOL_TAIL = 4 * V7X_SUBLANES


def _window_sum(blk, win, n):
    lo_off = -(win // 2)
    if win <= 4:
        acc = blk[POOL_HALO + lo_off:POOL_HALO + lo_off + n, :]
        for off in range(lo_off + 1, win + lo_off):
            acc = acc + blk[POOL_HALO + off:POOL_HALO + off + n, :]
        return acc
    levels = win.bit_length() - 1
    spare = V7X_SUBLANES * levels
    p = blk[POOL_HALO + lo_off:POOL_HALO + lo_off + n + spare, :]
    span = 1
    while span < win:
        spare -= V7X_SUBLANES
        p = p[0:n + spare, :] + p[span:span + n + spare, :]
        span *= 2
    return p


def _pool_kernel(u_ref, w_ref, sc_ref, o_ref, pad_ref, *, S):
    grp = pl.program_id(1)
    RC = ROW_CHUNK
    HALO = POOL_HALO
    gd = POOL_GROUP_DIM
    pad_ref[0:HALO, :] = jnp.zeros((HALO, gd), F32)
    pad_ref[HALO + S:HALO + S + POOL_TAIL, :] = jnp.zeros((POOL_TAIL, gd), F32)
    pad_ref[HALO:HALO + S, :] = u_ref[0]
    w16 = w_ref[0].astype(BF16)
    scale = sc_ref[...]

    for gi, win in enumerate(POOL_WINDOWS):
        lo_off = -(win // 2)
        hi_off = win - win // 2

        @pl.when(grp == gi)
        def _():
            def body(i, carry):
                r0 = pl.multiple_of(i * RC, RC)
                blk = pad_ref[pl.ds(r0, RC + HALO + POOL_TAIL), :]
                acc = _window_sum(blk, win, RC)
                pos = r0 + lax.broadcasted_iota(jnp.int32, (RC, 1), 0)
                cnt = jnp.minimum(pos + hi_off, S) - jnp.maximum(pos + lo_off, 0)
                pooled = acc / cnt.astype(F32) - blk[HALO:HALO + RC, :]
                mixed = _dot(pooled.astype(BF16), w16) * scale
                o_ref[0, pl.ds(r0, RC), :] = mixed.astype(o_ref.dtype)
                return carry

            lax.fori_loop(0, S // RC, body, 0, unroll=2)


def _pool(u, pool_w, pool_scale):
    B, S, _ = u.shape
    gd = POOL_GROUP_DIM
    return pl.pallas_call(
        functools.partial(_pool_kernel, S=S),
        grid=(B, len(POOL_WINDOWS)),
        in_specs=[
            pl.BlockSpec((1, S, gd), lambda b, g: (b, 0, g)),
            pl.BlockSpec((1, gd, gd), lambda b, g: (g, 0, 0)),
            pl.BlockSpec((1, gd), lambda b, g: (0, g)),
        ],
        out_specs=pl.BlockSpec((1, S, gd), lambda b, g: (b, 0, g)),
        out_shape=jax.ShapeDtypeStruct((B, S, POOL_WIDTH), BF16),
        scratch_shapes=[pltpu.VMEM((S + POOL_HALO + POOL_TAIL, gd), F32)],
        compiler_params=_params("parallel", "parallel"),
        name="pool",
    )(u, pool_w, pool_scale)


def _dilated_kernel(q_ref, k_ref, v_ref, o_ref,
                    qd_ref, kd_ref, vd_ref, od_ref, ld_ref, on_ref, ln_ref, num_ref, den_ref, mx_ref, *, S):
    grp = pl.program_id(2)
    QB = DA_QBLOCK
    W = V7X_LANES
    n_it = S // QB
    lane = lax.broadcasted_iota(jnp.int32, (1, W), 1)
    pair_mask = [((lane // (DA_HEAD_DIM // 2)) % 2 == e) for e in range(2)]
    first_head = lane < DA_HEAD_DIM
    qi2 = lax.broadcasted_iota(jnp.int32, (2 * QB, 2 * QB), 0) % QB
    kj2 = lax.broadcasted_iota(jnp.int32, (2 * QB, 2 * QB), 1)
    kcol = lax.broadcasted_iota(jnp.int32, (1, 2 * QB), 1)

    for gi, (window, dil) in enumerate(DA_CONFIGS):
        radius = window // (2 * dil)
        assert radius == QB // 2
        L = S // dil
        LP = L + QB
        blocks_per_res = L // QB
        band2 = jnp.abs(kj2 - radius - qi2) <= radius

        @pl.when(grp == gi)
        def _():
            zpad = jnp.zeros((radius, W), BF16)
            for r in range(dil):
                src = pl.ds(r, L, stride=dil) if dil > 1 else pl.ds(0, L)
                qd_ref[r * L:(r + 1) * L, :] = q_ref[0, src, :].astype(BF16)
                base = r * LP
                kd_ref[base:base + radius, :] = zpad
                vd_ref[base:base + radius, :] = zpad
                kd_ref[base + radius:base + radius + L, :] = k_ref[0, src, :].astype(BF16)
                vd_ref[base + radius:base + radius + L, :] = v_ref[0, src, :].astype(BF16)
                kd_ref[base + radius + L:base + LP, :] = zpad
                vd_ref[base + radius + L:base + LP, :] = zpad

            def body(step, carry):
                blocks = []
                for j in range(DA_LOCKSTEP):
                    it = step * DA_LOCKSTEP + j
                    res = it // blocks_per_res
                    m0 = (it % blocks_per_res) * QB
                    q0 = pl.multiple_of(it * QB, QB)
                    k0 = pl.multiple_of(it * QB + res * QB, QB)
                    kpos = kcol + (m0 - radius)
                    qp = qd_ref[pl.ds(q0, QB), :]
                    zero = jnp.zeros_like(qp)
                    blocks.append(dict(
                        q0=q0, valid=band2 & (kpos >= 0) & (kpos < L),
                        q=jnp.concatenate([jnp.where(pair_mask[e], qp, zero) for e in range(2)], axis=0),
                        k=kd_ref[pl.ds(k0, 2 * QB), :], v=vd_ref[pl.ds(k0, 2 * QB), :]))
                scores = [jnp.where(b["valid"], _dot_nt(b["q"], b["k"]), MASK_VALUE) for b in blocks]
                mxs = [jnp.max(s, axis=-1, keepdims=True) for s in scores]
                probs = [jnp.exp(s - m) for s, m in zip(scores, mxs)]
                dens = [jnp.sum(p, axis=-1, keepdims=True) for p in probs]
                outs = [_dot(p.astype(BF16), b["v"]) / d for p, b, d in zip(probs, blocks, dens)]
                for b, o, m, d in zip(blocks, outs, mxs, dens):
                    lse = m + jnp.log(d)
                    od_ref[pl.ds(b["q0"], QB), :] = jnp.where(first_head, o[:QB], o[QB:])
                    ld_ref[pl.ds(b["q0"], QB), :] = jnp.where(first_head, lse[:QB], lse[QB:])
                return carry

            lax.fori_loop(0, n_it // DA_LOCKSTEP, body, 0)

            for r in range(dil):
                dst = pl.ds(r, L, stride=dil) if dil > 1 else pl.ds(0, L)
                on_ref[dst, :] = od_ref[r * L:(r + 1) * L, :]
                ln_ref[dst, :] = ld_ref[r * L:(r + 1) * L, :]

    @pl.when(grp == 0)
    def _():
        num_ref[...] = on_ref[...]
        den_ref[...] = jnp.ones_like(den_ref)
        mx_ref[...] = ln_ref[...]

    @pl.when(grp > 0)
    def _():
        m_old = mx_ref[...]
        lse = ln_ref[...]
        m_new = jnp.maximum(m_old, lse)
        a_old = jnp.exp(m_old - m_new)
        a_new = jnp.exp(lse - m_new)
        num_ref[...] = num_ref[...] * a_old + on_ref[...] * a_new
        den_ref[...] = den_ref[...] * a_old + a_new
        mx_ref[...] = m_new

    @pl.when(grp == DA_NGROUPS - 1)
    def _():
        o_ref[0] = (num_ref[...] / den_ref[...]).astype(o_ref.dtype)


def _dilated(daq, dak, dav):
    B, S, _ = daq.shape
    W = V7X_LANES
    pairs = DA_GROUP_WIDTH // W
    max_dil = max(d for _, d in DA_CONFIGS)
    grp_spec = pl.BlockSpec((1, S, W), lambda b, p, g: (b, 0, g * pairs + p))
    full = lambda dt: pltpu.VMEM((S, W), dt)
    return pl.pallas_call(
        functools.partial(_dilated_kernel, S=S),
        grid=(B, pairs, DA_NGROUPS),
        in_specs=[grp_spec, grp_spec, grp_spec],
        out_specs=pl.BlockSpec((1, S, W), lambda b, p, g: (b, 0, p)),
        out_shape=jax.ShapeDtypeStruct((B, S, DA_GROUP_WIDTH), BF16),
        scratch_shapes=[
            full(BF16),
            pltpu.VMEM((S + max_dil * DA_QBLOCK, W), BF16),
            pltpu.VMEM((S + max_dil * DA_QBLOCK, W), BF16),
            full(F32), full(F32), full(F32), full(F32), full(F32), full(F32), full(F32),
        ],
        compiler_params=_params("parallel", "parallel", "arbitrary"),
        name="dilated",
    )(daq, dak, dav)


def _mix_out_kernel(x_ref, gain_ref, oa_ref, z_ref, onorm_ref, yb_ref, yc_ref, wg_ref, bg_ref, wa_ref, wb_ref,
                    wc_ref, wo_ref, o_ref):
    x = x_ref[...]
    D = x.shape[-1]
    h = _rms(x, gain_ref[...]).astype(BF16)
    o = oa_ref[...]
    z = z_ref[...]
    heads = []
    for hd in range(DN_HEADS):
        cols = slice(hd * DN_HEAD_DIM, (hd + 1) * DN_HEAD_DIM)
        zh = z[:, cols]
        heads.append((_rms(o[:, cols], onorm_ref[...]) * (zh * _sigmoid(zh))).astype(BF16))
    ya = jnp.concatenate(heads, axis=1)
    merged = jnp.zeros(x.shape, F32)
    for i, (y, w_ref) in enumerate(((ya, wa_ref), (yb_ref[...], wb_ref), (yc_ref[...], wc_ref))):
        gate = _sigmoid(_dot(h, wg_ref[:, i * D:(i + 1) * D]) + bg_ref[:, i * D:(i + 1) * D])
        merged = merged + gate * _dot(y, w_ref[...])
    o_ref[...] = x + _dot(merged.astype(BF16), wo_ref[...])


def _mix_out(x, gain, oa, z, onorm, yb, yc, wg, bg, wa, wb, wc, wo):
    T, D = x.shape
    tm = MIX_TOKEN_TILE
    tok = lambda n: pl.BlockSpec((tm, n), lambda i: (i, 0))
    return pl.pallas_call(
        _mix_out_kernel,
        grid=(T // tm,),
        in_specs=[
            tok(D), _resident((1, D)),
            tok(oa.shape[1]), tok(z.shape[1]), _resident(onorm.shape), tok(yb.shape[1]), tok(yc.shape[1]),
            _resident(wg.shape), _resident(bg.shape), _resident(wa.shape), _resident(wb.shape),
            _resident(wc.shape), _resident(wo.shape),
        ],
        out_specs=tok(D),
        out_shape=jax.ShapeDtypeStruct((T, D), F32),
        compiler_params=_params("parallel"),
        name="mix_out",
    )(x, gain, oa, z, onorm, yb, yc, wg, bg, wa, wb, wc, wo)


def _pair_layout_columns():
    half = DA_HEAD_DIM // 2
    order = []
    for g in range(DA_NGROUPS):
        for p in range(DA_HEADS_PER_GROUP // 2):
            for part in range(2):
                for e in range(2):
                    head = 2 * p + e
                    start = g * DA_GROUP_WIDTH + head * DA_HEAD_DIM + part * half
                    order.extend(range(start, start + half))
    return jnp.asarray(order, dtype=jnp.int32)


def _rope_tables(seq_len):
    half = DA_HEAD_DIM // 2
    inv_freq = ROPE_THETA ** (-jnp.arange(half, dtype=F32) / half)
    ang = jnp.arange(seq_len).astype(F32)[:, None] * inv_freq[None, :]
    cos = jnp.tile(jnp.cos(ang), (1, V7X_LANES // half))
    sin = jnp.sin(ang)
    sin = jnp.concatenate([-sin, -sin, sin, sin], axis=1)
    return cos, sin


def kernel(x, ffn1_norm, ffn1_w_gate, ffn1_w_up, ffn1_w_down, mix_norm, w_in, dn_conv, dn_a_log, dn_dt_bias, dn_out_norm, pool_w, pool_scale, w_proj_a, w_proj_b, w_proj_c, w_gate, b_gate, w_out, ffn2_norm, ffn2_w_gate, ffn2_w_up, ffn2_w_down, final_norm):
    B, S, D = x.shape
    depth = w_in.shape[0]
    T = B * S
    off_z = 3 * DN_WIDTH
    off_beta = off_z + DN_WIDTH
    off_pool = off_beta + 4 * DN_HEADS
    off_da = off_pool + POOL_WIDTH
    pair_cols = _pair_layout_columns()
    cos, sin = _rope_tables(S)
    fgain = final_norm.reshape(1, D)

    xt = x.reshape(T, D)
    for l in range(depth):
        xt = _ffn(xt, ffn1_norm[l].reshape(1, D), ffn1_w_gate[l].astype(BF16), ffn1_w_up[l].astype(BF16),
                  ffn1_w_down[l].astype(BF16), fgain, final_norm=False)

        w = w_in[l]
        w_da = w[:, off_da:]
        w_da = jnp.concatenate([w_da[:, :DA_WIDTH][:, pair_cols],
                                w_da[:, DA_WIDTH:2 * DA_WIDTH][:, pair_cols],
                                w_da[:, 2 * DA_WIDTH:]], axis=1)
        w_ba = jnp.pad(w[:, off_beta:off_pool], ((0, 0), (0, V7X_LANES - 4 * DN_HEADS)))
        qkv, z, bg, u, daq, dak, dav = _mix_in(
            xt, mix_norm[l].reshape(1, D), w[:, :off_z].astype(BF16), w[:, off_z:off_beta].astype(BF16),
            w_ba.astype(BF16), w[:, off_pool:off_da].astype(BF16), w_da.astype(BF16), dn_conv[l],
            dn_a_log[l], dn_dt_bias[l], cos, sin, S)

        o_a = _deltanet(qkv.reshape(B, S, -1), bg.reshape(B, S, -1))
        y_b = _pool(u.reshape(B, S, -1), pool_w[l], pool_scale[l].reshape(1, POOL_WIDTH))
        y_c = _dilated(daq.reshape(B, S, -1), dak.reshape(B, S, -1), dav.reshape(B, S, -1))

        xt = _mix_out(xt, mix_norm[l].reshape(1, D), o_a.reshape(T, -1), z,
                      dn_out_norm[l].reshape(1, DN_HEAD_DIM), y_b.reshape(T, -1),
                      y_c.reshape(T, -1), w_gate[l].astype(BF16), b_gate[l].reshape(1, -1),
                      w_proj_a[l].astype(BF16), w_proj_b[l].astype(BF16), w_proj_c[l].astype(BF16),
                      w_out[l].astype(BF16))

        xt = _ffn(xt, ffn2_norm[l].reshape(1, D), ffn2_w_gate[l].astype(BF16), ffn2_w_up[l].astype(BF16),
                  ffn2_w_down[l].astype(BF16), fgain, final_norm=(l == depth - 1))
    return xt.reshape(B, S, D)
```

```python
import functools

import jax
import jax.numpy as jnp
from jax import lax
from jax.experimental import pallas as pl
from jax.experimental.pallas import tpu as pltpu

F32 = jnp.float32
BF16 = jnp.bfloat16

RMS_EPS = 1e-6
L2_EPS = 1e-6

DN_HEADS = 4
DN_HEAD_DIM = 128
DN_WIDTH = DN_HEADS * DN_HEAD_DIM
DN_CONV = 5
DN_CHUNK = 128
DN_PREP_CHUNKS = 8
DN_HEADS_PER_STEP = 2

POOL_WINDOWS = (2, 4, 8, 16)
POOL_GROUP_DIM = 128
POOL_WIDTH = len(POOL_WINDOWS) * POOL_GROUP_DIM

DA_CONFIGS = ((128, 1), (512, 4), (2048, 16))
DA_NGROUPS = len(DA_CONFIGS)
DA_HEADS_PER_GROUP = 4
DA_HEAD_DIM = 64
DA_GROUP_WIDTH = DA_HEADS_PER_GROUP * DA_HEAD_DIM
DA_WIDTH = DA_NGROUPS * DA_GROUP_WIDTH
DA_QBLOCK = 128
DA_LOCKSTEP = 8
ROPE_THETA = 10000.0
MASK_VALUE = -1e30

N_BRANCHES = 3

V7X_LANES = 128
V7X_SUBLANES = 8
V7X_VMEM_LIMIT_BYTES = 56 * 1024 * 1024

V7X_MXU_WIDTH = 256

FFN_TOKEN_TILE = 512
MIX_TOKEN_TILE = 512
ROW_CHUNK = 256


def _params(*semantics):
    return pltpu.CompilerParams(dimension_semantics=semantics,
                                vmem_limit_bytes=V7X_VMEM_LIMIT_BYTES)


def _resident(shape):
    nd = len(shape)
    return pl.BlockSpec(shape, lambda *_: (0,) * nd, pipeline_mode=pl.Buffered(1))


def _rms(x, gain):
    return x * lax.rsqrt(jnp.mean(x * x, axis=-1, keepdims=True) + RMS_EPS) * gain


def _sigmoid(x):
    return 0.5 * jnp.tanh(0.5 * x) + 0.5


def _dot(a, b):
    return jnp.dot(a, b, preferred_element_type=F32)


def _dot_nt(a, b):
    return lax.dot_general(a, b, (((1,), (1,)), ((), ())), preferred_element_type=F32)


def _dot_tn(a, b):
    return lax.dot_general(a, b, (((0,), (0,)), ((), ())), preferred_element_type=F32)


def _mm(a, b):
    return _dot(a.astype(BF16), b.astype(BF16))


def _ffn_kernel(x_ref, gain_ref, wg_ref, wu_ref, wd_ref, fgain_ref, o_ref, *, final_norm):
    x = x_ref[...]
    h = _rms(x, gain_ref[...]).astype(BF16)
    F = wg_ref.shape[1]
    acc = jnp.zeros(x.shape, F32)
    split = (F // (2 * V7X_MXU_WIDTH)) * V7X_MXU_WIDTH
    for cols in (slice(0, split), slice(split, F)):
        g = _dot(h, wg_ref[:, cols])
        u = _dot(h, wu_ref[:, cols])
        a = (g * _sigmoid(g) * u).astype(BF16)
        acc = acc + _dot(a, wd_ref[cols, :])
    y = x + 0.5 * acc
    if final_norm:
        y = _rms(y, fgain_ref[...])
    o_ref[...] = y


def _ffn(x, gain, wg, wu, wd, fgain, *, final_norm):
    T, D = x.shape
    tm = FFN_TOKEN_TILE
    return pl.pallas_call(
        functools.partial(_ffn_kernel, final_norm=final_norm),
        grid=(T // tm,),
        in_specs=[
            pl.BlockSpec((tm, D), lambda i: (i, 0)),
            _resident((1, D)),
            _resident(wg.shape), _resident(wu.shape), _resident(wd.shape),
            _resident((1, D)),
        ],
        out_specs=pl.BlockSpec((tm, D), lambda i: (i, 0)),
        out_shape=jax.ShapeDtypeStruct((T, D), F32),
        compiler_params=_params("parallel"),
        name="ffn",
    )(x, gain, wg, wu, wd, fgain)


def _mix_in_kernel(x_ref, xp_ref, xn_ref, gain_ref, wqkv_ref, wz_ref, wba_ref, wpool_ref, wda_ref,
                   conv_ref, alog_ref, dtb_ref, cos_ref, sin_ref,
                   qkv_ref, z_ref, bg_ref, u_ref, daq_ref, dak_ref, dav_ref, *, tiles_per_seq):
    tm = x_ref.shape[0]
    HALO = V7X_SUBLANES
    half = DN_CONV // 2
    tile = pl.program_id(0) % tiles_per_seq
    gain = gain_ref[...]
    h = _rms(x_ref[...], gain).astype(BF16)

    h_halo = _rms(jnp.concatenate([xp_ref[...], xn_ref[...]], axis=0), gain).astype(BF16)
    wqkv = wqkv_ref[...]
    halo = _dot(h_halo, wqkv)
    prev = jnp.where(tile > 0, halo[:HALO], 0.0)
    nxt = jnp.where(tile < tiles_per_seq - 1, halo[HALO:], 0.0)
    win = jnp.concatenate([prev, _dot(h, wqkv), nxt], axis=0)
    cw = conv_ref[...]
    for cb in range(3 * DN_HEADS):
        cols = slice(cb * DN_HEAD_DIM, (cb + 1) * DN_HEAD_DIM)
        wc = win[:, cols]
        acc = jnp.zeros((tm, DN_HEAD_DIM), F32)
        for j in range(DN_CONV):
            off = HALO + j - half
            acc = acc + wc[off:off + tm, :] * cw[j:j + 1, cols]
        y = acc * _sigmoid(acc)
        if cb < 2 * DN_HEADS:
            y = y * lax.rsqrt(jnp.sum(y * y, axis=-1, keepdims=True) + L2_EPS)
        if cb < DN_HEADS:
            y = y * DN_HEAD_DIM ** -0.5
        qkv_ref[:, cols] = y

    z_ref[...] = _dot(h, wz_ref[...])

    ba = _dot(h, wba_ref[...])
    lane = lax.broadcasted_iota(jnp.int32, (1, V7X_LANES), 1)
    a_raw = ba + dtb_ref[...]
    softplus = jnp.maximum(a_raw, 0.0) + jnp.log(1.0 + jnp.exp(-jnp.abs(a_raw)))
    bg_ref[...] = jnp.where(lane < 2 * DN_HEADS, _sigmoid(ba), -jnp.exp(alog_ref[...]) * softplus)

    u_ref[...] = _dot(h, wpool_ref[...])
    da = _dot(h, wda_ref[...])
    cos = cos_ref[...]
    sin = sin_ref[...]
    for part, out_ref, scale in ((0, daq_ref, DA_HEAD_DIM ** -0.5), (1, dak_ref, 1.0)):
        for cb in range(DA_WIDTH // V7X_LANES):
            lo = part * DA_WIDTH + cb * V7X_LANES
            t = da[:, lo:lo + V7X_LANES]
            r = t * cos + pltpu.roll(t, V7X_LANES // 2, 1) * sin
            out_ref[:, cb * V7X_LANES:(cb + 1) * V7X_LANES] = r * scale
    dav_ref[...] = da[:, 2 * DA_WIDTH:]


def _lane_row(p):
    return jnp.pad(p.reshape(1, -1), ((0, 0), (2 * DN_HEADS, V7X_LANES - 4 * DN_HEADS)))


def _mix_in(x, gain, wqkv, wz, wba, wpool, wda, conv_w, a_log, dt_bias, cos, sin, seq_len):
    T, D = x.shape
    tm = MIX_TOKEN_TILE
    tiles_per_seq = seq_len // tm
    hb = tm // V7X_SUBLANES
    tok = lambda n: pl.BlockSpec((tm, n), lambda i: (i, 0))
    return pl.pallas_call(
        functools.partial(_mix_in_kernel, tiles_per_seq=tiles_per_seq),
        grid=(T // tm,),
        in_specs=[
            tok(D),
            pl.BlockSpec((V7X_SUBLANES, D), lambda i: (jnp.maximum(i * hb - 1, 0), 0)),
            pl.BlockSpec((V7X_SUBLANES, D), lambda i: (jnp.minimum((i + 1) * hb, T // V7X_SUBLANES - 1), 0)),
            _resident((1, D)),
            _resident(wqkv.shape), _resident(wz.shape), _resident(wba.shape),
            _resident(wpool.shape), _resident(wda.shape),
            _resident(conv_w.shape), _resident((1, V7X_LANES)), _resident((1, V7X_LANES)),
            pl.BlockSpec((tm, V7X_LANES), lambda i: (i % tiles_per_seq, 0)),
            pl.BlockSpec((tm, V7X_LANES), lambda i: (i % tiles_per_seq, 0)),
        ],
        out_specs=[tok(3 * DN_WIDTH), tok(DN_WIDTH), tok(V7X_LANES), tok(POOL_WIDTH),
                   tok(DA_WIDTH), tok(DA_WIDTH), tok(DA_WIDTH)],
        out_shape=[
            jax.ShapeDtypeStruct((T, 3 * DN_WIDTH), F32),
            jax.ShapeDtypeStruct((T, DN_WIDTH), F32),
            jax.ShapeDtypeStruct((T, V7X_LANES), F32),
            jax.ShapeDtypeStruct((T, POOL_WIDTH), F32),
            jax.ShapeDtypeStruct((T, DA_WIDTH), F32),
            jax.ShapeDtypeStruct((T, DA_WIDTH), F32),
            jax.ShapeDtypeStruct((T, DA_WIDTH), F32),
        ],
        compiler_params=_params("parallel"),
        name="mix_in",
    )(x, x, x, gain, wqkv, wz, wba, wpool, wda, conv_w, _lane_row(a_log), _lane_row(dt_bias), cos, sin)


def _unit_tri_inverse(mats, eye, ri, ci, n):
    blk = 2
    same = (ri // blk) == (ci // blk)
    ts = [eye - jnp.where(same, a, 0.0) for a in mats]
    while blk < n:
        joins = ((ri // (2 * blk)) == (ci // (2 * blk))) & ((ri // blk) != (ci // blk))
        ets = [_mm(jnp.where(joins, a, 0.0), t) for a, t in zip(mats, ts)]
        ts = [t - _mm(t, et) for t, et in zip(ts, ets)]
        blk *= 2
    return ts


def _deltanet_kernel(q_ref, k_ref, v_ref, bg_ref, o_ref,
                     beta_ref, gc_ref, u_ref, wq_ref, qk_ref, kd_ref, gl_ref, *, S, C):
    assert C == V7X_LANES
    RC = ROW_CHUNK
    n_rc = S // RC
    NH = DN_HEADS_PER_STEP

    lane = lax.broadcasted_iota(jnp.int32, (1, V7X_LANES), 1)
    rr = lax.broadcasted_iota(jnp.int32, (RC, RC), 0)
    rc = lax.broadcasted_iota(jnp.int32, (RC, RC), 1)
    same_chunk = (rr // C) == (rc // C)
    cum_mat = [(same_chunk & (rr >= rc)).astype(BF16), (same_chunk & (rr <= rc)).astype(BF16)]
    GRP = 4 * DN_HEADS
    sel_row = lax.broadcasted_iota(jnp.int32, (V7X_LANES, V7X_LANES), 0)

    def gate_body(i, carry):
        r0 = pl.multiple_of(i * RC, RC)
        bg = bg_ref[0, pl.ds(r0, RC), :]
        g_hi = bg.astype(BF16).astype(F32)
        g_mid = (bg - g_hi).astype(BF16).astype(F32)
        g_lo = (bg - g_hi) - g_mid
        parts = jnp.where(lane < GRP, g_hi,
                          jnp.where(lane < 2 * GRP, pltpu.roll(g_mid, GRP, 1),
                                    jnp.where(lane < 3 * GRP, pltpu.roll(g_lo, 2 * GRP, 1), 0.0))).astype(BF16)

        def pick(x, sel):
            col = jnp.sum(jnp.where(sel, x, 0.0), axis=-1, keepdims=True)
            return jnp.broadcast_to(col, (RC, DN_HEAD_DIM))

        for d in range(2):
            cum = _dot(cum_mat[d], parts)
            for hh in range(NH):
                head = pl.program_id(1) * NH + hh
                cb = d * DN_HEADS + head
                beta_sel = ((sel_row == cb) | (sel_row == cb + GRP) | (sel_row == cb + 2 * GRP)).astype(BF16)
                beta_ref[2 * hh + d, pl.ds(r0, RC), :] = _dot(parts, beta_sel)
                c = 2 * DN_HEADS + d * DN_HEADS + head
                gc_ref[2 * hh + d, pl.ds(r0, RC), :] = pick(
                    cum, (lane == c) | (lane == c + GRP) | (lane == c + 2 * GRP))
        return carry

    lax.fori_loop(0, n_rc, gate_body, 0, unroll=2)

    ri = lax.broadcasted_iota(jnp.int32, (C, C), 0)
    ci = lax.broadcasted_iota(jnp.int32, (C, C), 1)
    eye = (ri == ci).astype(F32)
    incl = [ri >= ci, ri <= ci]
    strict = [ri > ci, ri < ci]
    n_chunks = S // C

    def chunk_rows(c_idx, n):
        return pl.ds(pl.multiple_of(c_idx * n, n), n)

    def prepare(hh, chunks):
        cols = slice(hh * DN_HEAD_DIM, (hh + 1) * DN_HEAD_DIM)
        st = []
        for c_idx in chunks:
            rows = chunk_rows(c_idx, C)
            k = k_ref[0, rows, cols]
            v = v_ref[0, rows, cols]
            beta = [beta_ref[2 * hh + d, rows, :] for d in range(2)]
            gc = [gc_ref[2 * hh + d, rows, :] for d in range(2)]
            st.append(dict(
                c=c_idx, rows=rows, q=q_ref[0, rows, cols], k=k, gc=gc,
                kb=[k * b for b in beta], vb=[v * b for b in beta],
                decay=[jnp.where(incl[d], jnp.exp(gc[d] - gc[d].T), 0.0) for d in range(2)],
                g_last=[gc[0][C - 1:C, :], gc[1][0:1, :]]))
        kq = [_dot_nt(jnp.concatenate(s["kb"] + [s["q"]], axis=0).astype(BF16), s["k"].astype(BF16)) for s in st]
        a = [jnp.where(strict[d], m[d * C:(d + 1) * C] * s["decay"][d], 0.0)
             for s, m in zip(st, kq) for d in range(2)]
        t = _unit_tri_inverse(a, eye, ri, ci, C)
        eg = [jnp.exp(s["gc"][d]) for s in st for d in range(2)]
        sol = [_mm(t[2 * n + d], jnp.concatenate([s["vb"][d], s["kb"][d] * eg[2 * n + d]], axis=1))
               for n, s in enumerate(st) for d in range(2)]
        for n, (s, m) in enumerate(zip(st, kq)):
            for d in range(2):
                x, e = sol[2 * n + d], eg[2 * n + d]
                i, c_idx, rows = 2 * hh + d, s["c"], s["rows"]
                u_ref[i, rows, :] = x[:, :DN_HEAD_DIM]
                wq_ref[i, chunk_rows(c_idx, 2 * C), :] = jnp.concatenate(
                    [x[:, DN_HEAD_DIM:], s["q"] * e], axis=0).astype(BF16)
                qk_ref[i, rows, :] = (m[2 * C:] * s["decay"][d]).astype(BF16)
                kd_ref[i, rows, :] = (s["k"] * jnp.exp(s["g_last"][d] - s["gc"][d])).astype(BF16)
                gl_ref[i, chunk_rows(c_idx, V7X_SUBLANES), :] = jnp.broadcast_to(
                    jnp.exp(s["g_last"][d]), (V7X_SUBLANES, DN_HEAD_DIM))

    for hh in range(NH):
        def prepare_body(i, carry, hh=hh):
            prepare(hh, [i * DN_PREP_CHUNKS + j for j in range(DN_PREP_CHUNKS)])
            return carry

        lax.fori_loop(0, n_chunks // DN_PREP_CHUNKS, prepare_body, 0)

    chains = [(hh, d) for hh in range(NH) for d in range(2)]

    def chunk_body(c, states, *, accumulate):
        idx = (c, n_chunks - 1 - c)
        s16 = [s.astype(BF16) for s in states]
        ws = [_dot(wq_ref[2 * hh + d, chunk_rows(idx[d], 2 * C), :], s16[n])
              for n, (hh, d) in enumerate(chains)]
        v_new = [(u_ref[2 * hh + d, chunk_rows(idx[d], C), :] - ws[n][:C]).astype(BF16)
                 for n, (hh, d) in enumerate(chains)]
        for n, (hh, d) in enumerate(chains):
            o = ws[n][C:] + _dot(qk_ref[2 * hh + d, chunk_rows(idx[d], C), :], v_new[n])
            rows = chunk_rows(idx[d], C)
            cols = slice(hh * DN_HEAD_DIM, (hh + 1) * DN_HEAD_DIM)
            if accumulate:
                o_ref[0, rows, cols] += o
            else:
                o_ref[0, rows, cols] = o
        return tuple(
            states[n] * gl_ref[2 * hh + d, chunk_rows(idx[d], V7X_SUBLANES), :][0:1, :]
            + _dot_tn(kd_ref[2 * hh + d, chunk_rows(idx[d], C), :], v_new[n])
            for n, (hh, d) in enumerate(chains))

    assert n_chunks % 2 == 0
    zero_state = jnp.zeros((DN_HEAD_DIM, DN_HEAD_DIM), F32)
    states = lax.fori_loop(0, n_chunks // 2, functools.partial(chunk_body, accumulate=False),
                           (zero_state,) * len(chains))
    lax.fori_loop(n_chunks // 2, n_chunks, functools.partial(chunk_body, accumulate=True), states)


def _deltanet(qkv, bg):
    B, S, _ = qkv.shape
    C = DN_CHUNK
    hd = DN_HEAD_DIM
    NH = DN_HEADS_PER_STEP
    steps = DN_HEADS // NH
    seq = lambda off: pl.BlockSpec((1, S, NH * hd), lambda b, h: (b, 0, off + h))
    lanes = NH * 2
    return pl.pallas_call(
        functools.partial(_deltanet_kernel, S=S, C=C),
        grid=(B, steps),
        in_specs=[
            seq(0), seq(steps), seq(2 * steps),
            pl.BlockSpec((1, S, V7X_LANES), lambda b, h: (b, 0, 0)),
        ],
        out_specs=pl.BlockSpec((1, S, NH * hd), lambda b, h: (b, 0, h)),
        out_shape=jax.ShapeDtypeStruct((B, S, DN_WIDTH), F32),
        scratch_shapes=[
            pltpu.VMEM((lanes, S, hd), F32), pltpu.VMEM((lanes, S, hd), F32),
            pltpu.VMEM((lanes, S, hd), F32),
            pltpu.VMEM((lanes, 2 * S, hd), BF16),
            pltpu.VMEM((lanes, S, C), BF16),
            pltpu.VMEM((lanes, S, hd), BF16),
            pltpu.VMEM((lanes, (S // C) * V7X_SUBLANES, hd), F32),
        ],
        compiler_params=_params("parallel", "parallel"),
        name="deltanet",
    )(qkv, qkv, qkv, bg)


POOL_HALO = 2 * V7X_SUBLANES
POOL_TAIL = 4 * V7X_SUBLANES


def _window_sum(blk, win, n):
    lo_off = -(win // 2)
    if win <= 4:
        acc = blk[POOL_HALO + lo_off:POOL_HALO + lo_off + n, :]
        for off in range(lo_off + 1, win + lo_off):
            acc = acc + blk[POOL_HALO + off:POOL_HALO + off + n, :]
        return acc
    levels = win.bit_length() - 1
    spare = V7X_SUBLANES * levels
    p = blk[POOL_HALO + lo_off:POOL_HALO + lo_off + n + spare, :]
    span = 1
    while span < win:
        spare -= V7X_SUBLANES
        p = p[0:n + spare, :] + p[span:span + n + spare, :]
        span *= 2
    return p


def _pool_kernel(u_ref, w_ref, sc_ref, o_ref, pad_ref, *, S):
    grp = pl.program_id(1)
    RC = ROW_CHUNK
    HALO = POOL_HALO
    gd = POOL_GROUP_DIM
    pad_ref[0:HALO, :] = jnp.zeros((HALO, gd), F32)
    pad_ref[HALO + S:HALO + S + POOL_TAIL, :] = jnp.zeros((POOL_TAIL, gd), F32)
    pad_ref[HALO:HALO + S, :] = u_ref[0]
    w16 = w_ref[0].astype(BF16)
    scale = sc_ref[...]

    for gi, win in enumerate(POOL_WINDOWS):
        lo_off = -(win // 2)
        hi_off = win - win // 2

        @pl.when(grp == gi)
        def _():
            def body(i, carry):
                r0 = pl.multiple_of(i * RC, RC)
                blk = pad_ref[pl.ds(r0, RC + HALO + POOL_TAIL), :]
                acc = _window_sum(blk, win, RC)
                pos = r0 + lax.broadcasted_iota(jnp.int32, (RC, 1), 0)
                cnt = jnp.minimum(pos + hi_off, S) - jnp.maximum(pos + lo_off, 0)
                pooled = acc / cnt.astype(F32) - blk[HALO:HALO + RC, :]
                mixed = _dot(pooled.astype(BF16), w16) * scale
                o_ref[0, pl.ds(r0, RC), :] = mixed.astype(o_ref.dtype)
                return carry

            lax.fori_loop(0, S // RC, body, 0, unroll=2)


def _pool(u, pool_w, pool_scale):
    B, S, _ = u.shape
    gd = POOL_GROUP_DIM
    return pl.pallas_call(
        functools.partial(_pool_kernel, S=S),
        grid=(B, len(POOL_WINDOWS)),
        in_specs=[
            pl.BlockSpec((1, S, gd), lambda b, g: (b, 0, g)),
            pl.BlockSpec((1, gd, gd), lambda b, g: (g, 0, 0)),
            pl.BlockSpec((1, gd), lambda b, g: (0, g)),
        ],
        out_specs=pl.BlockSpec((1, S, gd), lambda b, g: (b, 0, g)),
        out_shape=jax.ShapeDtypeStruct((B, S, POOL_WIDTH), BF16),
        scratch_shapes=[pltpu.VMEM((S + POOL_HALO + POOL_TAIL, gd), F32)],
        compiler_params=_params("parallel", "parallel"),
        name="pool",
    )(u, pool_w, pool_scale)


def _dilated_kernel(q_ref, k_ref, v_ref, o_ref,
                    qd_ref, kd_ref, vd_ref, od_ref, ld_ref, on_ref, ln_ref, num_ref, den_ref, mx_ref, *, S):
    grp = pl.program_id(2)
    QB = DA_QBLOCK
    W = V7X_LANES
    n_it = S // QB
    lane = lax.broadcasted_iota(jnp.int32, (1, W), 1)
    pair_mask = [((lane // (DA_HEAD_DIM // 2)) % 2 == e) for e in range(2)]
    first_head = lane < DA_HEAD_DIM
    qi2 = lax.broadcasted_iota(jnp.int32, (2 * QB, 2 * QB), 0) % QB
    kj2 = lax.broadcasted_iota(jnp.int32, (2 * QB, 2 * QB), 1)
    kcol = lax.broadcasted_iota(jnp.int32, (1, 2 * QB), 1)

    for gi, (window, dil) in enumerate(DA_CONFIGS):
        radius = window // (2 * dil)
        assert radius == QB // 2
        L = S // dil
        LP = L + QB
        blocks_per_res = L // QB
        band2 = jnp.abs(kj2 - radius - qi2) <= radius

        @pl.when(grp == gi)
        def _():
            zpad = jnp.zeros((radius, W), BF16)
            for r in range(dil):
                src = pl.ds(r, L, stride=dil) if dil > 1 else pl.ds(0, L)
                qd_ref[r * L:(r + 1) * L, :] = q_ref[0, src, :].astype(BF16)
                base = r * LP
                kd_ref[base:base + radius, :] = zpad
                vd_ref[base:base + radius, :] = zpad
                kd_ref[base + radius:base + radius + L, :] = k_ref[0, src, :].astype(BF16)
                vd_ref[base + radius:base + radius + L, :] = v_ref[0, src, :].astype(BF16)
                kd_ref[base + radius + L:base + LP, :] = zpad
                vd_ref[base + radius + L:base + LP, :] = zpad

            def body(step, carry):
                blocks = []
                for j in range(DA_LOCKSTEP):
                    it = step * DA_LOCKSTEP + j
                    res = it // blocks_per_res
                    m0 = (it % blocks_per_res) * QB
                    q0 = pl.multiple_of(it * QB, QB)
                    k0 = pl.multiple_of(it * QB + res * QB, QB)
                    kpos = kcol + (m0 - radius)
                    qp = qd_ref[pl.ds(q0, QB), :]
                    zero = jnp.zeros_like(qp)
                    blocks.append(dict(
                        q0=q0, valid=band2 & (kpos >= 0) & (kpos < L),
                        q=jnp.concatenate([jnp.where(pair_mask[e], qp, zero) for e in range(2)], axis=0),
                        k=kd_ref[pl.ds(k0, 2 * QB), :], v=vd_ref[pl.ds(k0, 2 * QB), :]))
                scores = [jnp.where(b["valid"], _dot_nt(b["q"], b["k"]), MASK_VALUE) for b in blocks]
                mxs = [jnp.max(s, axis=-1, keepdims=True) for s in scores]
                probs = [jnp.exp(s - m) for s, m in zip(scores, mxs)]
                dens = [jnp.sum(p, axis=-1, keepdims=True) for p in probs]
                outs = [_dot(p.astype(BF16), b["v"]) / d for p, b, d in zip(probs, blocks, dens)]
                for b, o, m, d in zip(blocks, outs, mxs, dens):
                    lse = m + jnp.log(d)
                    od_ref[pl.ds(b["q0"], QB), :] = jnp.where(first_head, o[:QB], o[QB:])
                    ld_ref[pl.ds(b["q0"], QB), :] = jnp.where(first_head, lse[:QB], lse[QB:])
                return carry

            lax.fori_loop(0, n_it // DA_LOCKSTEP, body, 0)

            for r in range(dil):
                dst = pl.ds(r, L, stride=dil) if dil > 1 else pl.ds(0, L)
                on_ref[dst, :] = od_ref[r * L:(r + 1) * L, :]
                ln_ref[dst, :] = ld_ref[r * L:(r + 1) * L, :]

    @pl.when(grp == 0)
    def _():
        num_ref[...] = on_ref[...]
        den_ref[...] = jnp.ones_like(den_ref)
        mx_ref[...] = ln_ref[...]

    @pl.when(grp > 0)
    def _():
        m_old = mx_ref[...]
        lse = ln_ref[...]
        m_new = jnp.maximum(m_old, lse)
        a_old = jnp.exp(m_old - m_new)
        a_new = jnp.exp(lse - m_new)
        num_ref[...] = num_ref[...] * a_old + on_ref[...] * a_new
        den_ref[...] = den_ref[...] * a_old + a_new
        mx_ref[...] = m_new

    @pl.when(grp == DA_NGROUPS - 1)
    def _():
        o_ref[0] = (num_ref[...] / den_ref[...]).astype(o_ref.dtype)


def _dilated(daq, dak, dav):
    B, S, _ = daq.shape
    W = V7X_LANES
    pairs = DA_GROUP_WIDTH // W
    max_dil = max(d for _, d in DA_CONFIGS)
    grp_spec = pl.BlockSpec((1, S, W), lambda b, p, g: (b, 0, g * pairs + p))
    full = lambda dt: pltpu.VMEM((S, W), dt)
    return pl.pallas_call(
        functools.partial(_dilated_kernel, S=S),
        grid=(B, pairs, DA_NGROUPS),
        in_specs=[grp_spec, grp_spec, grp_spec],
        out_specs=pl.BlockSpec((1, S, W), lambda b, p, g: (b, 0, p)),
        out_shape=jax.ShapeDtypeStruct((B, S, DA_GROUP_WIDTH), BF16),
        scratch_shapes=[
            full(BF16),
            pltpu.VMEM((S + max_dil * DA_QBLOCK, W), BF16),
            pltpu.VMEM((S + max_dil * DA_QBLOCK, W), BF16),
            full(F32), full(F32), full(F32), full(F32), full(F32), full(F32), full(F32),
        ],
        compiler_params=_params("parallel", "parallel", "arbitrary"),
        name="dilated",
    )(daq, dak, dav)


def _mix_out_kernel(x_ref, gain_ref, oa_ref, z_ref, onorm_ref, yb_ref, yc_ref, wg_ref, bg_ref, wa_ref, wb_ref,
                    wc_ref, wo_ref, o_ref):
    x = x_ref[...]
    D = x.shape[-1]
    h = _rms(x, gain_ref[...]).astype(BF16)
    o = oa_ref[...]
    z = z_ref[...]
    heads = []
    for hd in range(DN_HEADS):
        cols = slice(hd * DN_HEAD_DIM, (hd + 1) * DN_HEAD_DIM)
        zh = z[:, cols]
        heads.append((_rms(o[:, cols], onorm_ref[...]) * (zh * _sigmoid(zh))).astype(BF16))
    ya = jnp.concatenate(heads, axis=1)
    merged = jnp.zeros(x.shape, F32)
    for i, (y, w_ref) in enumerate(((ya, wa_ref), (yb_ref[...], wb_ref), (yc_ref[...], wc_ref))):
        gate = _sigmoid(_dot(h, wg_ref[:, i * D:(i + 1) * D]) + bg_ref[:, i * D:(i + 1) * D])
        merged = merged + gate * _dot(y, w_ref[...])
    o_ref[...] = x + _dot(merged.astype(BF16), wo_ref[...])


def _mix_out(x, gain, oa, z, onorm, yb, yc, wg, bg, wa, wb, wc, wo):
    T, D = x.shape
    tm = MIX_TOKEN_TILE
    tok = lambda n: pl.BlockSpec((tm, n), lambda i: (i, 0))
    return pl.pallas_call(
        _mix_out_kernel,
        grid=(T // tm,),
        in_specs=[
            tok(D), _resident((1, D)),
            tok(oa.shape[1]), tok(z.shape[1]), _resident(onorm.shape), tok(yb.shape[1]), tok(yc.shape[1]),
            _resident(wg.shape), _resident(bg.shape), _resident(wa.shape), _resident(wb.shape),
            _resident(wc.shape), _resident(wo.shape),
        ],
        out_specs=tok(D),
        out_shape=jax.ShapeDtypeStruct((T, D), F32),
        compiler_params=_params("parallel"),
        name="mix_out",
    )(x, gain, oa, z, onorm, yb, yc, wg, bg, wa, wb, wc, wo)


def _pair_layout_columns():
    half = DA_HEAD_DIM // 2
    order = []
    for g in range(DA_NGROUPS):
        for p in range(DA_HEADS_PER_GROUP // 2):
            for part in range(2):
                for e in range(2):
                    head = 2 * p + e
                    start = g * DA_GROUP_WIDTH + head * DA_HEAD_DIM + part * half
                    order.extend(range(start, start + half))
    return jnp.asarray(order, dtype=jnp.int32)


def _rope_tables(seq_len):
    half = DA_HEAD_DIM // 2
    inv_freq = ROPE_THETA ** (-jnp.arange(half, dtype=F32) / half)
    ang = jnp.arange(seq_len).astype(F32)[:, None] * inv_freq[None, :]
    cos = jnp.tile(jnp.cos(ang), (1, V7X_LANES // half))
    sin = jnp.sin(ang)
    sin = jnp.concatenate([-sin, -sin, sin, sin], axis=1)
    return cos, sin


def kernel(x, ffn1_norm, ffn1_w_gate, ffn1_w_up, ffn1_w_down, mix_norm, w_in, dn_conv, dn_a_log, dn_dt_bias, dn_out_norm, pool_w, pool_scale, w_proj_a, w_proj_b, w_proj_c, w_gate, b_gate, w_out, ffn2_norm, ffn2_w_gate, ffn2_w_up, ffn2_w_down, final_norm):
    B, S, D = x.shape
    depth = w_in.shape[0]
    T = B * S
    off_z = 3 * DN_WIDTH
    off_beta = off_z + DN_WIDTH
    off_pool = off_beta + 4 * DN_HEADS
    off_da = off_pool + POOL_WIDTH
    pair_cols = _pair_layout_columns()
    cos, sin = _rope_tables(S)
    fgain = final_norm.reshape(1, D)

    xt = x.reshape(T, D)
    for l in range(depth):
        xt = _ffn(xt, ffn1_norm[l].reshape(1, D), ffn1_w_gate[l].astype(BF16), ffn1_w_up[l].astype(BF16),
                  ffn1_w_down[l].astype(BF16), fgain, final_norm=False)

        w = w_in[l]
        w_da = w[:, off_da:]
        w_da = jnp.concatenate([w_da[:, :DA_WIDTH][:, pair_cols],
                                w_da[:, DA_WIDTH:2 * DA_WIDTH][:, pair_cols],
                                w_da[:, 2 * DA_WIDTH:]], axis=1)
        w_ba = jnp.pad(w[:, off_beta:off_pool], ((0, 0), (0, V7X_LANES - 4 * DN_HEADS)))
        qkv, z, bg, u, daq, dak, dav = _mix_in(
            xt, mix_norm[l].reshape(1, D), w[:, :off_z].astype(BF16), w[:, off_z:off_beta].astype(BF16),
            w_ba.astype(BF16), w[:, off_pool:off_da].astype(BF16), w_da.astype(BF16), dn_conv[l],
            dn_a_log[l], dn_dt_bias[l], cos, sin, S)

        o_a = _deltanet(qkv.reshape(B, S, -1), bg.reshape(B, S, -1))
        y_b = _pool(u.reshape(B, S, -1), pool_w[l], pool_scale[l].reshape(1, POOL_WIDTH))
        y_c = _dilated(daq.reshape(B, S, -1), dak.reshape(B, S, -1), dav.reshape(B, S, -1))

        xt = _mix_out(xt, mix_norm[l].reshape(1, D), o_a.reshape(T, -1), z,
                      dn_out_norm[l].reshape(1, DN_HEAD_DIM), y_b.reshape(T, -1),
                      y_c.reshape(T, -1), w_gate[l].astype(BF16), b_gate[l].reshape(1, -1),
                      w_proj_a[l].astype(BF16), w_proj_b[l].astype(BF16), w_proj_c[l].astype(BF16),
                      w_out[l].astype(BF16))

        xt = _ffn(xt, ffn2_norm[l].reshape(1, D), ffn2_w_gate[l].astype(BF16), ffn2_w_up[l].astype(BF16),
                  ffn2_w_down[l].astype(BF16), fgain, final_norm=(l == depth - 1))
    return xt.reshape(B, S, D)
```

```python
import functools

import jax
import jax.numpy as jnp
from jax import lax
from jax.experimental import pallas as pl
from jax.experimental.pallas import tpu as pltpu

F32 = jnp.float32
BF16 = jnp.bfloat16

RMS_EPS = 1e-6
L2_EPS = 1e-6

DN_HEADS = 4
DN_HEAD_DIM = 128
DN_WIDTH = DN_HEADS * DN_HEAD_DIM
DN_CONV = 5
DN_CHUNK = 128
DN_PREP_CHUNKS = 8
DN_HEADS_PER_STEP = 2

POOL_WINDOWS = (2, 4, 8, 16)
POOL_GROUP_DIM = 128
POOL_WIDTH = len(POOL_WINDOWS) * POOL_GROUP_DIM

DA_CONFIGS = ((128, 1), (512, 4), (2048, 16))
DA_NGROUPS = len(DA_CONFIGS)
DA_HEADS_PER_GROUP = 4
DA_HEAD_DIM = 64
DA_GROUP_WIDTH = DA_HEADS_PER_GROUP * DA_HEAD_DIM
DA_WIDTH = DA_NGROUPS * DA_GROUP_WIDTH
DA_QBLOCK = 128
DA_LOCKSTEP = 8
ROPE_THETA = 10000.0
MASK_VALUE = -1e30

N_BRANCHES = 3

V7X_LANES = 128
V7X_SUBLANES = 8
V7X_VMEM_LIMIT_BYTES = 56 * 1024 * 1024

V7X_MXU_WIDTH = 256

FFN_TOKEN_TILE = 512
MIX_TOKEN_TILE = 512
ROW_CHUNK = 256


def _params(*semantics):
    return pltpu.CompilerParams(dimension_semantics=semantics,
                                vmem_limit_bytes=V7X_VMEM_LIMIT_BYTES)


def _resident(shape):
    nd = len(shape)
    return pl.BlockSpec(shape, lambda *_: (0,) * nd, pipeline_mode=pl.Buffered(1))


def _rms(x, gain):
    return x * lax.rsqrt(jnp.mean(x * x, axis=-1, keepdims=True) + RMS_EPS) * gain


def _sigmoid(x):
    return 0.5 * jnp.tanh(0.5 * x) + 0.5


def _silu(x):
    h = 0.5 * x
    return h + h * jnp.tanh(h)


def _dot(a, b):
    return jnp.dot(a, b, preferred_element_type=F32)


def _dot_nt(a, b):
    return lax.dot_general(a, b, (((1,), (1,)), ((), ())), preferred_element_type=F32)


def _dot_tn(a, b):
    return lax.dot_general(a, b, (((0,), (0,)), ((), ())), preferred_element_type=F32)


def _mm(a, b):
    return _dot(a.astype(BF16), b.astype(BF16))


def _ffn_kernel(x_ref, gain_ref, wg_ref, wu_ref, wd_ref, fgain_ref, o_ref, *, final_norm):
    x = x_ref[...]
    h = _rms(x, gain_ref[...]).astype(BF16)
    F = wg_ref.shape[1]
    acc = jnp.zeros(x.shape, F32)
    split = (F // (2 * V7X_MXU_WIDTH)) * V7X_MXU_WIDTH
    for cols in (slice(0, split), slice(split, F)):
        g = _dot(h, wg_ref[:, cols])
        u = _dot(h, wu_ref[:, cols])
        a = (_silu(g) * u).astype(BF16)
        acc = acc + _dot(a, wd_ref[cols, :])
    y = x + 0.5 * acc
    if final_norm:
        y = _rms(y, fgain_ref[...])
    o_ref[...] = y


def _ffn(x, gain, wg, wu, wd, fgain, *, final_norm):
    T, D = x.shape
    tm = FFN_TOKEN_TILE
    return pl.pallas_call(
        functools.partial(_ffn_kernel, final_norm=final_norm),
        grid=(T // tm,),
        in_specs=[
            pl.BlockSpec((tm, D), lambda i: (i, 0)),
            _resident((1, D)),
            _resident(wg.shape), _resident(wu.shape), _resident(wd.shape),
            _resident((1, D)),
        ],
        out_specs=pl.BlockSpec((tm, D), lambda i: (i, 0)),
        out_shape=jax.ShapeDtypeStruct((T, D), F32),
        compiler_params=_params("parallel"),
        name="ffn",
    )(x, gain, wg, wu, wd, fgain)


def _mix_in_kernel(x_ref, xp_ref, xn_ref, gain_ref, wqkv_ref, wz_ref, wba_ref, wpool_ref, wda_ref,
                   conv_ref, alog_ref, dtb_ref, cos_ref, sin_ref,
                   qkv_ref, z_ref, bg_ref, u_ref, daq_ref, dak_ref, dav_ref, *, tiles_per_seq):
    tm = x_ref.shape[0]
    HALO = V7X_SUBLANES
    half = DN_CONV // 2
    tile = pl.program_id(0) % tiles_per_seq
    gain = gain_ref[...]
    h = _rms(x_ref[...], gain).astype(BF16)

    h_halo = _rms(jnp.concatenate([xp_ref[...], xn_ref[...]], axis=0), gain).astype(BF16)
    wqkv = wqkv_ref[...]
    halo = _dot(h_halo, wqkv)
    prev = jnp.where(tile > 0, halo[:HALO], 0.0)
    nxt = jnp.where(tile < tiles_per_seq - 1, halo[HALO:], 0.0)
    win = jnp.concatenate([prev, _dot(h, wqkv), nxt], axis=0)
    cw = conv_ref[...]
    for cb in range(3 * DN_HEADS):
        cols = slice(cb * DN_HEAD_DIM, (cb + 1) * DN_HEAD_DIM)
        wc = win[:, cols]
        acc = jnp.zeros((tm, DN_HEAD_DIM), F32)
        for j in range(DN_CONV):
            tap = wc if j == half else pltpu.roll(wc, (half - j) % (tm + 2 * HALO), 0)
            acc = acc + tap[HALO:HALO + tm, :] * cw[j:j + 1, cols]
        y = _silu(acc)
        if cb < 2 * DN_HEADS:
            inv = lax.rsqrt(jnp.sum(y * y, axis=-1, keepdims=True) + L2_EPS)
            y = y * (inv * DN_HEAD_DIM ** -0.5 if cb < DN_HEADS else inv)
        qkv_ref[:, cols] = y

    z_ref[...] = _dot(h, wz_ref[...])

    ba = _dot(h, wba_ref[...])
    lane = lax.broadcasted_iota(jnp.int32, (1, V7X_LANES), 1)
    a_raw = ba + dtb_ref[...]
    softplus = jnp.maximum(a_raw, 0.0) + jnp.log(1.0 + jnp.exp(-jnp.abs(a_raw)))
    bg_ref[...] = jnp.where(lane < 2 * DN_HEADS, _sigmoid(ba), -jnp.exp(alog_ref[...]) * softplus)

    u_ref[...] = _dot(h, wpool_ref[...])
    da = _dot(h, wda_ref[...])
    cos = cos_ref[...]
    sin = sin_ref[...]
    for part, out_ref, scale in ((0, daq_ref, DA_HEAD_DIM ** -0.5), (1, dak_ref, 1.0)):
        for cb in range(DA_WIDTH // V7X_LANES):
            lo = part * DA_WIDTH + cb * V7X_LANES
            t = da[:, lo:lo + V7X_LANES]
            r = t * cos + pltpu.roll(t, V7X_LANES // 2, 1) * sin
            out_ref[:, cb * V7X_LANES:(cb + 1) * V7X_LANES] = r if scale == 1.0 else r * scale
    dav_ref[...] = da[:, 2 * DA_WIDTH:]


def _lane_row(p):
    return jnp.pad(p.reshape(1, -1), ((0, 0), (2 * DN_HEADS, V7X_LANES - 4 * DN_HEADS)))


def _mix_in(x, gain, wqkv, wz, wba, wpool, wda, conv_w, a_log, dt_bias, cos, sin, seq_len):
    T, D = x.shape
    tm = MIX_TOKEN_TILE
    tiles_per_seq = seq_len // tm
    hb = tm // V7X_SUBLANES
    tok = lambda n: pl.BlockSpec((tm, n), lambda i: (i, 0))
    return pl.pallas_call(
        functools.partial(_mix_in_kernel, tiles_per_seq=tiles_per_seq),
        grid=(T // tm,),
        in_specs=[
            tok(D),
            pl.BlockSpec((V7X_SUBLANES, D), lambda i: (jnp.maximum(i * hb - 1, 0), 0)),
            pl.BlockSpec((V7X_SUBLANES, D), lambda i: (jnp.minimum((i + 1) * hb, T // V7X_SUBLANES - 1), 0)),
            _resident((1, D)),
            _resident(wqkv.shape), _resident(wz.shape), _resident(wba.shape),
            _resident(wpool.shape), _resident(wda.shape),
            _resident(conv_w.shape), _resident((1, V7X_LANES)), _resident((1, V7X_LANES)),
            pl.BlockSpec((tm, V7X_LANES), lambda i: (i % tiles_per_seq, 0)),
            pl.BlockSpec((tm, V7X_LANES), lambda i: (i % tiles_per_seq, 0)),
        ],
        out_specs=[tok(3 * DN_WIDTH), tok(DN_WIDTH), tok(V7X_LANES), tok(POOL_WIDTH),
                   tok(DA_WIDTH), tok(DA_WIDTH), tok(DA_WIDTH)],
        out_shape=[
            jax.ShapeDtypeStruct((T, 3 * DN_WIDTH), F32),
            jax.ShapeDtypeStruct((T, DN_WIDTH), F32),
            jax.ShapeDtypeStruct((T, V7X_LANES), F32),
            jax.ShapeDtypeStruct((T, POOL_WIDTH), F32),
            jax.ShapeDtypeStruct((T, DA_WIDTH), F32),
            jax.ShapeDtypeStruct((T, DA_WIDTH), F32),
            jax.ShapeDtypeStruct((T, DA_WIDTH), F32),
        ],
        compiler_params=_params("parallel"),
        name="mix_in",
    )(x, x, x, gain, wqkv, wz, wba, wpool, wda, conv_w, _lane_row(a_log), _lane_row(dt_bias), cos, sin)


def _unit_tri_inverse(mats, eye, ri, ci, n):
    blk = 2
    same = (ri // blk) == (ci // blk)
    ts = [eye - jnp.where(same, a, 0.0) for a in mats]
    while blk < n:
        joins = ((ri // (2 * blk)) == (ci // (2 * blk))) & ((ri // blk) != (ci // blk))
        ets = [_mm(jnp.where(joins, a, 0.0), t) for a, t in zip(mats, ts)]
        ts = [t - _mm(t, et) for t, et in zip(ts, ets)]
        blk *= 2
    return ts


def _deltanet_kernel(q_ref, k_ref, v_ref, bg_ref, o_ref,
                     beta_ref, gc_ref, u_ref, wq_ref, qk_ref, kd_ref, gl_ref, *, S, C):
    assert C == V7X_LANES
    RC = ROW_CHUNK
    n_rc = S // RC
    NH = DN_HEADS_PER_STEP

    lane = lax.broadcasted_iota(jnp.int32, (1, V7X_LANES), 1)
    rr = lax.broadcasted_iota(jnp.int32, (RC, RC), 0)
    rc = lax.broadcasted_iota(jnp.int32, (RC, RC), 1)
    same_chunk = (rr // C) == (rc // C)
    cum_mat = [(same_chunk & (rr >= rc)).astype(BF16), (same_chunk & (rr <= rc)).astype(BF16)]
    GRP = 4 * DN_HEADS
    sel_row = lax.broadcasted_iota(jnp.int32, (V7X_LANES, V7X_LANES), 0)

    def gate_body(i, carry):
        r0 = pl.multiple_of(i * RC, RC)
        bg = bg_ref[0, pl.ds(r0, RC), :]
        g_hi = bg.astype(BF16).astype(F32)
        g_mid = (bg - g_hi).astype(BF16).astype(F32)
        g_lo = (bg - g_hi) - g_mid
        parts = jnp.where(lane < GRP, g_hi,
                          jnp.where(lane < 2 * GRP, pltpu.roll(g_mid, GRP, 1),
                                    jnp.where(lane < 3 * GRP, pltpu.roll(g_lo, 2 * GRP, 1), 0.0))).astype(BF16)

        def pick(x, sel):
            col = jnp.sum(jnp.where(sel, x, 0.0), axis=-1, keepdims=True)
            return jnp.broadcast_to(col, (RC, DN_HEAD_DIM))

        for d in range(2):
            cum = _dot(cum_mat[d], parts)
            for hh in range(NH):
                head = pl.program_id(1) * NH + hh
                cb = d * DN_HEADS + head
                beta_sel = ((sel_row == cb) | (sel_row == cb + GRP) | (sel_row == cb + 2 * GRP)).astype(BF16)
                beta_ref[2 * hh + d, pl.ds(r0, RC), :] = _dot(parts, beta_sel)
                c = 2 * DN_HEADS + d * DN_HEADS + head
                gc_ref[2 * hh + d, pl.ds(r0, RC), :] = pick(
                    cum, (lane == c) | (lane == c + GRP) | (lane == c + 2 * GRP))
        return carry

    lax.fori_loop(0, n_rc, gate_body, 0, unroll=2)

    ri = lax.broadcasted_iota(jnp.int32, (C, C), 0)
    ci = lax.broadcasted_iota(jnp.int32, (C, C), 1)
    eye = (ri == ci).astype(F32)
    incl = [ri >= ci, ri <= ci]
    strict = [ri > ci, ri < ci]
    n_chunks = S // C

    def chunk_rows(c_idx, n):
        return pl.ds(pl.multiple_of(c_idx * n, n), n)

    def prepare(hh, chunks):
        cols = slice(hh * DN_HEAD_DIM, (hh + 1) * DN_HEAD_DIM)
        st = []
        for c_idx in chunks:
            rows = chunk_rows(c_idx, C)
            k = k_ref[0, rows, cols]
            v = v_ref[0, rows, cols]
            beta = [beta_ref[2 * hh + d, rows, :] for d in range(2)]
            gc = [gc_ref[2 * hh + d, rows, :] for d in range(2)]
            st.append(dict(
                c=c_idx, rows=rows, q=q_ref[0, rows, cols], k=k, gc=gc,
                kb=[k * b for b in beta], vb=[v * b for b in beta],
                decay=[jnp.where(incl[d], jnp.exp(gc[d] - gc[d].T), 0.0) for d in range(2)],
                g_last=[gc[0][C - 1:C, :], gc[1][0:1, :]]))
        kq = [_dot_nt(jnp.concatenate(s["kb"] + [s["q"]], axis=0).astype(BF16), s["k"].astype(BF16)) for s in st]
        a = [jnp.where(strict[d], m[d * C:(d + 1) * C] * s["decay"][d], 0.0)
             for s, m in zip(st, kq) for d in range(2)]
        t = _unit_tri_inverse(a, eye, ri, ci, C)
        eg = [jnp.exp(s["gc"][d]) for s in st for d in range(2)]
        sol = [_mm(t[2 * n + d], jnp.concatenate([s["vb"][d], s["kb"][d] * eg[2 * n + d]], axis=1))
               for n, s in enumerate(st) for d in range(2)]
        for n, (s, m) in enumerate(zip(st, kq)):
            for d in range(2):
                x, e = sol[2 * n + d], eg[2 * n + d]
                i, c_idx, rows = 2 * hh + d, s["c"], s["rows"]
                u_ref[i, rows, :] = x[:, :DN_HEAD_DIM]
                wq_ref[i, chunk_rows(c_idx, 2 * C), :] = jnp.concatenate(
                    [x[:, DN_HEAD_DIM:], s["q"] * e], axis=0).astype(BF16)
                qk_ref[i, rows, :] = (m[2 * C:] * s["decay"][d]).astype(BF16)
                kd_ref[i, rows, :] = (s["k"] * jnp.exp(s["g_last"][d] - s["gc"][d])).astype(BF16)
                gl_ref[i, chunk_rows(c_idx, V7X_SUBLANES), :] = jnp.broadcast_to(
                    jnp.exp(s["g_last"][d]), (V7X_SUBLANES, DN_HEAD_DIM))

    for hh in range(NH):
        def prepare_body(i, carry, hh=hh):
            prepare(hh, [i * DN_PREP_CHUNKS + j for j in range(DN_PREP_CHUNKS)])
            return carry

        lax.fori_loop(0, n_chunks // DN_PREP_CHUNKS, prepare_body, 0)

    chains = [(hh, d) for hh in range(NH) for d in range(2)]

    def chunk_body(c, states, *, accumulate):
        idx = (c, n_chunks - 1 - c)
        s16 = [s.astype(BF16) for s in states]
        ws = [_dot(wq_ref[2 * hh + d, chunk_rows(idx[d], 2 * C), :], s16[n])
              for n, (hh, d) in enumerate(chains)]
        v_new = [(u_ref[2 * hh + d, chunk_rows(idx[d], C), :] - ws[n][:C]).astype(BF16)
                 for n, (hh, d) in enumerate(chains)]
        for n, (hh, d) in enumerate(chains):
            o = ws[n][C:] + _dot(qk_ref[2 * hh + d, chunk_rows(idx[d], C), :], v_new[n])
            rows = chunk_rows(idx[d], C)
            cols = slice(hh * DN_HEAD_DIM, (hh + 1) * DN_HEAD_DIM)
            if accumulate:
                o_ref[0, rows, cols] += o
            else:
                o_ref[0, rows, cols] = o
        return tuple(
            states[n] * gl_ref[2 * hh + d, chunk_rows(idx[d], V7X_SUBLANES), :][0:1, :]
            + _dot_tn(kd_ref[2 * hh + d, chunk_rows(idx[d], C), :], v_new[n])
            for n, (hh, d) in enumerate(chains))

    assert n_chunks % 2 == 0
    zero_state = jnp.zeros((DN_HEAD_DIM, DN_HEAD_DIM), F32)
    states = lax.fori_loop(0, n_chunks // 2, functools.partial(chunk_body, accumulate=False),
                           (zero_state,) * len(chains))
    lax.fori_loop(n_chunks // 2, n_chunks, functools.partial(chunk_body, accumulate=True), states)


def _deltanet(qkv, bg):
    B, S, _ = qkv.shape
    C = DN_CHUNK
    hd = DN_HEAD_DIM
    NH = DN_HEADS_PER_STEP
    steps = DN_HEADS // NH
    seq = lambda off: pl.BlockSpec((1, S, NH * hd), lambda b, h: (b, 0, off + h))
    lanes = NH * 2
    return pl.pallas_call(
        functools.partial(_deltanet_kernel, S=S, C=C),
        grid=(B, steps),
        in_specs=[
            seq(0), seq(steps), seq(2 * steps),
            pl.BlockSpec((1, S, V7X_LANES), lambda b, h: (b, 0, 0)),
        ],
        out_specs=pl.BlockSpec((1, S, NH * hd), lambda b, h: (b, 0, h)),
        out_shape=jax.ShapeDtypeStruct((B, S, DN_WIDTH), F32),
        scratch_shapes=[
            pltpu.VMEM((lanes, S, hd), F32), pltpu.VMEM((lanes, S, hd), F32),
            pltpu.VMEM((lanes, S, hd), F32),
            pltpu.VMEM((lanes, 2 * S, hd), BF16),
            pltpu.VMEM((lanes, S, C), BF16),
            pltpu.VMEM((lanes, S, hd), BF16),
            pltpu.VMEM((lanes, (S // C) * V7X_SUBLANES, hd), F32),
        ],
        compiler_params=_params("parallel", "parallel"),
        name="deltanet",
    )(qkv, qkv, qkv, bg)


POOL_HALO = 2 * V7X_SUBLANES
POOL_TAIL = 4 * V7X_SUBLANES


def _window_sum(blk, win, n):
    lo_off = -(win // 2)
    if win <= 4:
        acc = blk[POOL_HALO + lo_off:POOL_HALO + lo_off + n, :]
        for off in range(lo_off + 1, win + lo_off):
            acc = acc + blk[POOL_HALO + off:POOL_HALO + off + n, :]
        return acc
    levels = win.bit_length() - 1
    spare = V7X_SUBLANES * levels
    p = blk[POOL_HALO + lo_off:POOL_HALO + lo_off + n + spare, :]
    span = 1
    while span < win:
        spare -= V7X_SUBLANES
        p = p[0:n + spare, :] + p[span:span + n + spare, :]
        span *= 2
    return p


def _pool_kernel(u_ref, w_ref, sc_ref, o_ref, pad_ref, *, S):
    grp = pl.program_id(1)
    RC = ROW_CHUNK
    HALO = POOL_HALO
    gd = POOL_GROUP_DIM
    pad_ref[0:HALO, :] = jnp.zeros((HALO, gd), F32)
    pad_ref[HALO + S:HALO + S + POOL_TAIL, :] = jnp.zeros((POOL_TAIL, gd), F32)
    pad_ref[HALO:HALO + S, :] = u_ref[0]
    w16 = w_ref[0].astype(BF16)
    scale = sc_ref[...]

    for gi, win in enumerate(POOL_WINDOWS):
        lo_off = -(win // 2)
        hi_off = win - win // 2

        @pl.when(grp == gi)
        def _():
            def body(i, carry):
                r0 = pl.multiple_of(i * RC, RC)
                blk = pad_ref[pl.ds(r0, RC + HALO + POOL_TAIL), :]
                acc = _window_sum(blk, win, RC)
                pos = r0 + lax.broadcasted_iota(jnp.int32, (RC, 1), 0)
                cnt = jnp.minimum(pos + hi_off, S) - jnp.maximum(pos + lo_off, 0)
                pooled = acc / cnt.astype(F32) - blk[HALO:HALO + RC, :]
                mixed = _dot(pooled.astype(BF16), w16) * scale
                o_ref[0, pl.ds(r0, RC), :] = mixed.astype(o_ref.dtype)
                return carry

            lax.fori_loop(0, S // RC, body, 0, unroll=2)


def _pool(u, pool_w, pool_scale):
    B, S, _ = u.shape
    gd = POOL_GROUP_DIM
    return pl.pallas_call(
        functools.partial(_pool_kernel, S=S),
        grid=(B, len(POOL_WINDOWS)),
        in_specs=[
            pl.BlockSpec((1, S, gd), lambda b, g: (b, 0, g)),
            pl.BlockSpec((1, gd, gd), lambda b, g: (g, 0, 0)),
            pl.BlockSpec((1, gd), lambda b, g: (0, g)),
        ],
        out_specs=pl.BlockSpec((1, S, gd), lambda b, g: (b, 0, g)),
        out_shape=jax.ShapeDtypeStruct((B, S, POOL_WIDTH), BF16),
        scratch_shapes=[pltpu.VMEM((S + POOL_HALO + POOL_TAIL, gd), F32)],
        compiler_params=_params("parallel", "parallel"),
        name="pool",
    )(u, pool_w, pool_scale)


def _dilated_kernel(q_ref, k_ref, v_ref, o_ref,
                    qd_ref, kd_ref, vd_ref, od_ref, ld_ref, on_ref, ln_ref, num_ref, den_ref, mx_ref, *, S):
    grp = pl.program_id(2)
    QB = DA_QBLOCK
    W = V7X_LANES
    n_it = S // QB
    lane = lax.broadcasted_iota(jnp.int32, (1, W), 1)
    pair_mask = [((lane // (DA_HEAD_DIM // 2)) % 2 == e) for e in range(2)]
    first_head = lane < DA_HEAD_DIM
    qi2 = lax.broadcasted_iota(jnp.int32, (2 * QB, 2 * QB), 0) % QB
    kj2 = lax.broadcasted_iota(jnp.int32, (2 * QB, 2 * QB), 1)
    kcol = lax.broadcasted_iota(jnp.int32, (1, 2 * QB), 1)

    for gi, (window, dil) in enumerate(DA_CONFIGS):
        radius = window // (2 * dil)
        assert radius == QB // 2
        L = S // dil
        LP = L + QB
        blocks_per_res = L // QB
        band2 = jnp.abs(kj2 - radius - qi2) <= radius

        @pl.when(grp == gi)
        def _():
            zpad = jnp.zeros((radius, W), BF16)
            for r in range(dil):
                src = pl.ds(r, L, stride=dil) if dil > 1 else pl.ds(0, L)
                qd_ref[r * L:(r + 1) * L, :] = q_ref[0, src, :].astype(BF16)
                base = r * LP
                kd_ref[base:base + radius, :] = zpad
                vd_ref[base:base + radius, :] = zpad
                kd_ref[base + radius:base + radius + L, :] = k_ref[0, src, :].astype(BF16)
                vd_ref[base + radius:base + radius + L, :] = v_ref[0, src, :].astype(BF16)
                kd_ref[base + radius + L:base + LP, :] = zpad
                vd_ref[base + radius + L:base + LP, :] = zpad

            def body(step, carry):
                blocks = []
                for j in range(DA_LOCKSTEP):
                    it = step * DA_LOCKSTEP + j
                    res = it // blocks_per_res
                    m0 = (it % blocks_per_res) * QB
                    q0 = pl.multiple_of(it * QB, QB)
                    k0 = pl.multiple_of(it * QB + res * QB, QB)
                    kpos = kcol + (m0 - radius)
                    qp = qd_ref[pl.ds(q0, QB), :]
                    zero = jnp.zeros_like(qp)
                    blocks.append(dict(
                        q0=q0, valid=band2 & (kpos >= 0) & (kpos < L),
                        q=jnp.concatenate([jnp.where(pair_mask[e], qp, zero) for e in range(2)], axis=0),
                        k=kd_ref[pl.ds(k0, 2 * QB), :], v=vd_ref[pl.ds(k0, 2 * QB), :]))
                scores = [jnp.where(b["valid"], _dot_nt(b["q"], b["k"]), MASK_VALUE) for b in blocks]
                mxs = [jnp.max(s, axis=-1, keepdims=True) for s in scores]
                probs = [jnp.exp(s - m) for s, m in zip(scores, mxs)]
                dens = [jnp.sum(p, axis=-1, keepdims=True) for p in probs]
                outs = [_dot(p.astype(BF16), b["v"]) / d for p, b, d in zip(probs, blocks, dens)]
                for b, o, m, d in zip(blocks, outs, mxs, dens):
                    lse = m + jnp.log(d)
                    od_ref[pl.ds(b["q0"], QB), :] = jnp.where(first_head, o[:QB], o[QB:])
                    ld_ref[pl.ds(b["q0"], QB), :] = jnp.where(first_head, lse[:QB], lse[QB:])
                return carry

            lax.fori_loop(0, n_it // DA_LOCKSTEP, body, 0)

            for r in range(dil):
                dst = pl.ds(r, L, stride=dil) if dil > 1 else pl.ds(0, L)
                on_ref[dst, :] = od_ref[r * L:(r + 1) * L, :]
                ln_ref[dst, :] = ld_ref[r * L:(r + 1) * L, :]

    @pl.when(grp == 0)
    def _():
        num_ref[...] = on_ref[...]
        den_ref[...] = jnp.ones_like(den_ref)
        mx_ref[...] = ln_ref[...]

    @pl.when(grp > 0)
    def _():
        m_old = mx_ref[...]
        lse = ln_ref[...]
        m_new = jnp.maximum(m_old, lse)
        a_old = jnp.exp(m_old - m_new)
        a_new = jnp.exp(lse - m_new)
        num_ref[...] = num_ref[...] * a_old + on_ref[...] * a_new
        den_ref[...] = den_ref[...] * a_old + a_new
        mx_ref[...] = m_new

    @pl.when(grp == DA_NGROUPS - 1)
    def _():
        o_ref[0] = (num_ref[...] / den_ref[...]).astype(o_ref.dtype)


def _dilated(daq, dak, dav):
    B, S, _ = daq.shape
    W = V7X_LANES
    pairs = DA_GROUP_WIDTH // W
    max_dil = max(d for _, d in DA_CONFIGS)
    grp_spec = pl.BlockSpec((1, S, W), lambda b, p, g: (b, 0, g * pairs + p))
    full = lambda dt: pltpu.VMEM((S, W), dt)
    return pl.pallas_call(
        functools.partial(_dilated_kernel, S=S),
        grid=(B, pairs, DA_NGROUPS),
        in_specs=[grp_spec, grp_spec, grp_spec],
        out_specs=pl.BlockSpec((1, S, W), lambda b, p, g: (b, 0, p)),
        out_shape=jax.ShapeDtypeStruct((B, S, DA_GROUP_WIDTH), BF16),
        scratch_shapes=[
            full(BF16),
            pltpu.VMEM((S + max_dil * DA_QBLOCK, W), BF16),
            pltpu.VMEM((S + max_dil * DA_QBLOCK, W), BF16),
            full(F32), full(F32), full(F32), full(F32), full(F32), full(F32), full(F32),
        ],
        compiler_params=_params("parallel", "parallel", "arbitrary"),
        name="dilated",
    )(daq, dak, dav)


def _mix_out_kernel(x_ref, gain_ref, oa_ref, z_ref, onorm_ref, yb_ref, yc_ref, wg_ref, bg_ref, wa_ref, wb_ref,
                    wc_ref, wo_ref, o_ref):
    x = x_ref[...]
    D = x.shape[-1]
    h = _rms(x, gain_ref[...]).astype(BF16)
    o = oa_ref[...]
    z = z_ref[...]
    heads = []
    for hd in range(DN_HEADS):
        cols = slice(hd * DN_HEAD_DIM, (hd + 1) * DN_HEAD_DIM)
        zh = z[:, cols]
        heads.append((_rms(o[:, cols], onorm_ref[...]) * _silu(zh)).astype(BF16))
    ya = jnp.concatenate(heads, axis=1)
    merged = jnp.zeros(x.shape, F32)
    for i, (y, w_ref) in enumerate(((ya, wa_ref), (yb_ref[...], wb_ref), (yc_ref[...], wc_ref))):
        gate = _sigmoid(_dot(h, wg_ref[:, i * D:(i + 1) * D]) + bg_ref[:, i * D:(i + 1) * D])
        merged = merged + gate * _dot(y, w_ref[...])
    o_ref[...] = x + _dot(merged.astype(BF16), wo_ref[...])


def _mix_out(x, gain, oa, z, onorm, yb, yc, wg, bg, wa, wb, wc, wo):
    T, D = x.shape
    tm = MIX_TOKEN_TILE
    tok = lambda n: pl.BlockSpec((tm, n), lambda i: (i, 0))
    return pl.pallas_call(
        _mix_out_kernel,
        grid=(T // tm,),
        in_specs=[
            tok(D), _resident((1, D)),
            tok(oa.shape[1]), tok(z.shape[1]), _resident(onorm.shape), tok(yb.shape[1]), tok(yc.shape[1]),
            _resident(wg.shape), _resident(bg.shape), _resident(wa.shape), _resident(wb.shape),
            _resident(wc.shape), _resident(wo.shape),
        ],
        out_specs=tok(D),
        out_shape=jax.ShapeDtypeStruct((T, D), F32),
        compiler_params=_params("parallel"),
        name="mix_out",
    )(x, gain, oa, z, onorm, yb, yc, wg, bg, wa, wb, wc, wo)


def _pair_layout_columns():
    half = DA_HEAD_DIM // 2
    order = []
    for g in range(DA_NGROUPS):
        for p in range(DA_HEADS_PER_GROUP // 2):
            for part in range(2):
                for e in range(2):
                    head = 2 * p + e
                    start = g * DA_GROUP_WIDTH + head * DA_HEAD_DIM + part * half
                    order.extend(range(start, start + half))
    return jnp.asarray(order, dtype=jnp.int32)


def _rope_tables(seq_len):
    half = DA_HEAD_DIM // 2
    inv_freq = ROPE_THETA ** (-jnp.arange(half, dtype=F32) / half)
    ang = jnp.arange(seq_len).astype(F32)[:, None] * inv_freq[None, :]
    cos = jnp.tile(jnp.cos(ang), (1, V7X_LANES // half))
    sin = jnp.sin(ang)
    sin = jnp.concatenate([-sin, -sin, sin, sin], axis=1)
    return cos, sin


def kernel(x, ffn1_norm, ffn1_w_gate, ffn1_w_up, ffn1_w_down, mix_norm, w_in, dn_conv, dn_a_log, dn_dt_bias, dn_out_norm, pool_w, pool_scale, w_proj_a, w_proj_b, w_proj_c, w_gate, b_gate, w_out, ffn2_norm, ffn2_w_gate, ffn2_w_up, ffn2_w_down, final_norm):
    B, S, D = x.shape
    depth = w_in.shape[0]
    T = B * S
    off_z = 3 * DN_WIDTH
    off_beta = off_z + DN_WIDTH
    off_pool = off_beta + 4 * DN_HEADS
    off_da = off_pool + POOL_WIDTH
    pair_cols = _pair_layout_columns()
    cos, sin = _rope_tables(S)
    fgain = final_norm.reshape(1, D)

    xt = x.reshape(T, D)
    for l in range(depth):
        xt = _ffn(xt, ffn1_norm[l].reshape(1, D), ffn1_w_gate[l].astype(BF16), ffn1_w_up[l].astype(BF16),
                  ffn1_w_down[l].astype(BF16), fgain, final_norm=False)

        w = w_in[l]
        w_da = w[:, off_da:]
        w_da = jnp.concatenate([w_da[:, :DA_WIDTH][:, pair_cols],
                                w_da[:, DA_WIDTH:2 * DA_WIDTH][:, pair_cols],
                                w_da[:, 2 * DA_WIDTH:]], axis=1)
        w_ba = jnp.pad(w[:, off_beta:off_pool], ((0, 0), (0, V7X_LANES - 4 * DN_HEADS)))
        qkv, z, bg, u, daq, dak, dav = _mix_in(
            xt, mix_norm[l].reshape(1, D), w[:, :off_z].astype(BF16), w[:, off_z:off_beta].astype(BF16),
            w_ba.astype(BF16), w[:, off_pool:off_da].astype(BF16), w_da.astype(BF16), dn_conv[l],
            dn_a_log[l], dn_dt_bias[l], cos, sin, S)

        o_a = _deltanet(qkv.reshape(B, S, -1), bg.reshape(B, S, -1))
        y_b = _pool(u.reshape(B, S, -1), pool_w[l], pool_scale[l].reshape(1, POOL_WIDTH))
        y_c = _dilated(daq.reshape(B, S, -1), dak.reshape(B, S, -1), dav.reshape(B, S, -1))

        xt = _mix_out(xt, mix_norm[l].reshape(1, D), o_a.reshape(T, -1), z,
                      dn_out_norm[l].reshape(1, DN_HEAD_DIM), y_b.reshape(T, -1),
                      y_c.reshape(T, -1), w_gate[l].astype(BF16), b_gate[l].reshape(1, -1),
                      w_proj_a[l].astype(BF16), w_proj_b[l].astype(BF16), w_proj_c[l].astype(BF16),
                      w_out[l].astype(BF16))

        xt = _ffn(xt, ffn2_norm[l].reshape(1, D), ffn2_w_gate[l].astype(BF16), ffn2_w_up[l].astype(BF16),
                  ffn2_w_down[l].astype(BF16), fgain, final_norm=(l == depth - 1))
    return xt.reshape(B, S, D)
```

```python
import functools

import jax
import jax.numpy as jnp
from jax import lax
from jax.experimental import pallas as pl
from jax.experimental.pallas import tpu as pltpu

F32 = jnp.float32
BF16 = jnp.bfloat16

RMS_EPS = 1e-6
L2_EPS = 1e-6

DN_HEADS = 4
DN_HEAD_DIM = 128
DN_WIDTH = DN_HEADS * DN_HEAD_DIM
DN_CONV = 5
DN_CHUNK = 128
DN_PREP_CHUNKS = 8
DN_HEADS_PER_STEP = 2

POOL_WINDOWS = (2, 4, 8, 16)
POOL_GROUP_DIM = 128
POOL_WIDTH = len(POOL_WINDOWS) * POOL_GROUP_DIM

DA_CONFIGS = ((128, 1), (512, 4), (2048, 16))
DA_NGROUPS = len(DA_CONFIGS)
DA_HEADS_PER_GROUP = 4
DA_HEAD_DIM = 64
DA_GROUP_WIDTH = DA_HEADS_PER_GROUP * DA_HEAD_DIM
DA_WIDTH = DA_NGROUPS * DA_GROUP_WIDTH
DA_QBLOCK = 128
DA_LOCKSTEP = 8
DA_MAX_STRIDE = 4
ROPE_THETA = 10000.0
MASK_VALUE = -1e30

N_BRANCHES = 3

V7X_LANES = 128
V7X_SUBLANES = 8
V7X_VMEM_LIMIT_BYTES = 56 * 1024 * 1024

V7X_MXU_WIDTH = 256

FFN_TOKEN_TILE = 512
MIX_TOKEN_TILE = 512
ROW_CHUNK = 256


def _params(*semantics):
    return pltpu.CompilerParams(dimension_semantics=semantics,
                                vmem_limit_bytes=V7X_VMEM_LIMIT_BYTES)


def _resident(shape):
    nd = len(shape)
    return pl.BlockSpec(shape, lambda *_: (0,) * nd, pipeline_mode=pl.Buffered(1))


def _rms(x, gain):
    return x * lax.rsqrt(jnp.mean(x * x, axis=-1, keepdims=True) + RMS_EPS) * gain


def _sigmoid(x):
    return 0.5 * jnp.tanh(0.5 * x) + 0.5


def _silu(x):
    h = 0.5 * x
    return h + h * jnp.tanh(h)


def _dot(a, b):
    return jnp.dot(a, b, preferred_element_type=F32)


def _dot_nt(a, b):
    return lax.dot_general(a, b, (((1,), (1,)), ((), ())), preferred_element_type=F32)


def _dot_tn(a, b):
    return lax.dot_general(a, b, (((0,), (0,)), ((), ())), preferred_element_type=F32)


def _mm(a, b):
    return _dot(a.astype(BF16), b.astype(BF16))


def _ffn_kernel(x_ref, gain_ref, wg_ref, wu_ref, wd_ref, fgain_ref, o_ref, *, final_norm):
    x = x_ref[...]
    h = _rms(x, gain_ref[...]).astype(BF16)
    F = wg_ref.shape[1]
    acc = jnp.zeros(x.shape, F32)
    split = (F // (2 * V7X_MXU_WIDTH)) * V7X_MXU_WIDTH
    for cols in (slice(0, split), slice(split, F)):
        g = _dot(h, wg_ref[:, cols])
        u = _dot(h, wu_ref[:, cols])
        a = (_silu(g) * u).astype(BF16)
        acc = acc + _dot(a, wd_ref[cols, :])
    y = x + 0.5 * acc
    if final_norm:
        y = _rms(y, fgain_ref[...])
    o_ref[...] = y


def _ffn(x, gain, wg, wu, wd, fgain, *, final_norm):
    T, D = x.shape
    tm = FFN_TOKEN_TILE
    return pl.pallas_call(
        functools.partial(_ffn_kernel, final_norm=final_norm),
        grid=(T // tm,),
        in_specs=[
            pl.BlockSpec((tm, D), lambda i: (i, 0)),
            _resident((1, D)),
            _resident(wg.shape), _resident(wu.shape), _resident(wd.shape),
            _resident((1, D)),
        ],
        out_specs=pl.BlockSpec((tm, D), lambda i: (i, 0)),
        out_shape=jax.ShapeDtypeStruct((T, D), F32),
        compiler_params=_params("parallel"),
        name="ffn",
    )(x, gain, wg, wu, wd, fgain)


def _mix_in_kernel(x_ref, xp_ref, xn_ref, gain_ref, wqkv_ref, wz_ref, wba_ref, wpool_ref, wda_ref,
                   conv_ref, alog_ref, dtb_ref, cos_ref, sin_ref,
                   qkv_ref, z_ref, bg_ref, u_ref, daq_ref, dak_ref, dav_ref, *, tiles_per_seq):
    tm = x_ref.shape[0]
    HALO = V7X_SUBLANES
    half = DN_CONV // 2
    tile = pl.program_id(0) % tiles_per_seq
    gain = gain_ref[...]
    h = _rms(x_ref[...], gain).astype(BF16)

    h_halo = _rms(jnp.concatenate([xp_ref[...], xn_ref[...]], axis=0), gain).astype(BF16)
    wqkv = wqkv_ref[...]
    halo = _dot(h_halo, wqkv)
    prev = jnp.where(tile > 0, halo[:HALO], 0.0)
    nxt = jnp.where(tile < tiles_per_seq - 1, halo[HALO:], 0.0)
    win = jnp.concatenate([prev, _dot(h, wqkv), nxt], axis=0)
    cw = conv_ref[...]
    for cb in range(3 * DN_HEADS):
        cols = slice(cb * DN_HEAD_DIM, (cb + 1) * DN_HEAD_DIM)
        wc = win[:, cols]
        acc = jnp.zeros((tm, DN_HEAD_DIM), F32)
        for j in range(DN_CONV):
            tap = wc if j == half else pltpu.roll(wc, (half - j) % (tm + 2 * HALO), 0)
            acc = acc + tap[HALO:HALO + tm, :] * cw[j:j + 1, cols]
        y = _silu(acc)
        if cb < 2 * DN_HEADS:
            inv = lax.rsqrt(jnp.sum(y * y, axis=-1, keepdims=True) + L2_EPS)
            y = y * (inv * DN_HEAD_DIM ** -0.5 if cb < DN_HEADS else inv)
        qkv_ref[:, cols] = y

    z_ref[...] = _dot(h, wz_ref[...])

    ba = _dot(h, wba_ref[...])
    lane = lax.broadcasted_iota(jnp.int32, (1, V7X_LANES), 1)
    a_raw = ba + dtb_ref[...]
    softplus = jnp.maximum(a_raw, 0.0) + jnp.log(1.0 + jnp.exp(-jnp.abs(a_raw)))
    bg_ref[...] = jnp.where(lane < 2 * DN_HEADS, _sigmoid(ba), -jnp.exp(alog_ref[...]) * softplus)

    u_ref[...] = _dot(h, wpool_ref[...])
    da = _dot(h, wda_ref[...])
    cos = cos_ref[...]
    sin = sin_ref[...]
    for part, out_ref, scale in ((0, daq_ref, DA_HEAD_DIM ** -0.5), (1, dak_ref, 1.0)):
        for cb in range(DA_WIDTH // V7X_LANES):
            lo = part * DA_WIDTH + cb * V7X_LANES
            t = da[:, lo:lo + V7X_LANES]
            r = t * cos + pltpu.roll(t, V7X_LANES // 2, 1) * sin
            out_ref[:, cb * V7X_LANES:(cb + 1) * V7X_LANES] = r if scale == 1.0 else r * scale
    dav_ref[...] = da[:, 2 * DA_WIDTH:]


def _lane_row(p):
    return jnp.pad(p.reshape(1, -1), ((0, 0), (2 * DN_HEADS, V7X_LANES - 4 * DN_HEADS)))


def _mix_in(x, gain, wqkv, wz, wba, wpool, wda, conv_w, a_log, dt_bias, cos, sin, seq_len):
    T, D = x.shape
    tm = MIX_TOKEN_TILE
    tiles_per_seq = seq_len // tm
    hb = tm // V7X_SUBLANES
    tok = lambda n: pl.BlockSpec((tm, n), lambda i: (i, 0))
    return pl.pallas_call(
        functools.partial(_mix_in_kernel, tiles_per_seq=tiles_per_seq),
        grid=(T // tm,),
        in_specs=[
            tok(D),
            pl.BlockSpec((V7X_SUBLANES, D), lambda i: (jnp.maximum(i * hb - 1, 0), 0)),
            pl.BlockSpec((V7X_SUBLANES, D), lambda i: (jnp.minimum((i + 1) * hb, T // V7X_SUBLANES - 1), 0)),
            _resident((1, D)),
            _resident(wqkv.shape), _resident(wz.shape), _resident(wba.shape),
            _resident(wpool.shape), _resident(wda.shape),
            _resident(conv_w.shape), _resident((1, V7X_LANES)), _resident((1, V7X_LANES)),
            pl.BlockSpec((tm, V7X_LANES), lambda i: (i % tiles_per_seq, 0)),
            pl.BlockSpec((tm, V7X_LANES), lambda i: (i % tiles_per_seq, 0)),
        ],
        out_specs=[tok(3 * DN_WIDTH), tok(DN_WIDTH), tok(V7X_LANES), tok(POOL_WIDTH),
                   tok(DA_WIDTH), tok(DA_WIDTH), tok(DA_WIDTH)],
        out_shape=[
            jax.ShapeDtypeStruct((T, 3 * DN_WIDTH), F32),
            jax.ShapeDtypeStruct((T, DN_WIDTH), F32),
            jax.ShapeDtypeStruct((T, V7X_LANES), F32),
            jax.ShapeDtypeStruct((T, POOL_WIDTH), F32),
            jax.ShapeDtypeStruct((T, DA_WIDTH), F32),
            jax.ShapeDtypeStruct((T, DA_WIDTH), F32),
            jax.ShapeDtypeStruct((T, DA_WIDTH), F32),
        ],
        compiler_params=_params("parallel"),
        name="mix_in",
    )(x, x, x, gain, wqkv, wz, wba, wpool, wda, conv_w, _lane_row(a_log), _lane_row(dt_bias), cos, sin)


def _unit_tri_inverse(mats, eye, ri, ci, n):
    blk = 2
    same = (ri // blk) == (ci // blk)
    ts = [eye - jnp.where(same, a, 0.0) for a in mats]
    while blk < n:
        joins = ((ri // (2 * blk)) == (ci // (2 * blk))) & ((ri // blk) != (ci // blk))
        ets = [_mm(jnp.where(joins, a, 0.0), t) for a, t in zip(mats, ts)]
        ts = [t - _mm(t, et) for t, et in zip(ts, ets)]
        blk *= 2
    return ts


def _deltanet_kernel(q_ref, k_ref, v_ref, bg_ref, o_ref,
                     beta_ref, gc_ref, u_ref, wq_ref, qk_ref, kd_ref, gl_ref, *, S, C):
    assert C == V7X_LANES
    RC = ROW_CHUNK
    n_rc = S // RC
    NH = DN_HEADS_PER_STEP

    lane = lax.broadcasted_iota(jnp.int32, (1, V7X_LANES), 1)
    rr = lax.broadcasted_iota(jnp.int32, (RC, RC), 0)
    rc = lax.broadcasted_iota(jnp.int32, (RC, RC), 1)
    same_chunk = (rr // C) == (rc // C)
    cum_mat = [(same_chunk & (rr >= rc)).astype(BF16), (same_chunk & (rr <= rc)).astype(BF16)]
    GRP = 4 * DN_HEADS
    sel_row = lax.broadcasted_iota(jnp.int32, (V7X_LANES, V7X_LANES), 0)

    def gate_body(i, carry):
        r0 = pl.multiple_of(i * RC, RC)
        bg = bg_ref[0, pl.ds(r0, RC), :]
        g_hi = bg.astype(BF16).astype(F32)
        g_mid = (bg - g_hi).astype(BF16).astype(F32)
        g_lo = (bg - g_hi) - g_mid
        parts = jnp.where(lane < GRP, g_hi,
                          jnp.where(lane < 2 * GRP, pltpu.roll(g_mid, GRP, 1),
                                    jnp.where(lane < 3 * GRP, pltpu.roll(g_lo, 2 * GRP, 1), 0.0))).astype(BF16)

        def pick(x, sel):
            col = jnp.sum(jnp.where(sel, x, 0.0), axis=-1, keepdims=True)
            return jnp.broadcast_to(col, (RC, DN_HEAD_DIM))

        for d in range(2):
            cum = _dot(cum_mat[d], parts)
            for hh in range(NH):
                head = pl.program_id(1) * NH + hh
                cb = d * DN_HEADS + head
                beta_sel = ((sel_row == cb) | (sel_row == cb + GRP) | (sel_row == cb + 2 * GRP)).astype(BF16)
                beta_ref[2 * hh + d, pl.ds(r0, RC), :] = _dot(parts, beta_sel)
                c = 2 * DN_HEADS + d * DN_HEADS + head
                gc_ref[2 * hh + d, pl.ds(r0, RC), :] = pick(
                    cum, (lane == c) | (lane == c + GRP) | (lane == c + 2 * GRP))
        return carry

    lax.fori_loop(0, n_rc, gate_body, 0, unroll=2)

    ri = lax.broadcasted_iota(jnp.int32, (C, C), 0)
    ci = lax.broadcasted_iota(jnp.int32, (C, C), 1)
    eye = (ri == ci).astype(F32)
    incl = [ri >= ci, ri <= ci]
    strict = [ri > ci, ri < ci]
    n_chunks = S // C

    def chunk_rows(c_idx, n):
        return pl.ds(pl.multiple_of(c_idx * n, n), n)

    def prepare(hh, chunks):
        cols = slice(hh * DN_HEAD_DIM, (hh + 1) * DN_HEAD_DIM)
        st = []
        for c_idx in chunks:
            rows = chunk_rows(c_idx, C)
            k = k_ref[0, rows, cols]
            v = v_ref[0, rows, cols]
            beta = [beta_ref[2 * hh + d, rows, :] for d in range(2)]
            gc = [gc_ref[2 * hh + d, rows, :] for d in range(2)]
            st.append(dict(
                c=c_idx, rows=rows, q=q_ref[0, rows, cols], k=k, gc=gc,
                kb=[k * b for b in beta], vb=[v * b for b in beta],
                decay=[jnp.where(incl[d], jnp.exp(gc[d] - gc[d].T), 0.0) for d in range(2)],
                g_last=[gc[0][C - 1:C, :], gc[1][0:1, :]]))
        kq = [_dot_nt(jnp.concatenate(s["kb"] + [s["q"]], axis=0).astype(BF16), s["k"].astype(BF16)) for s in st]
        a = [jnp.where(strict[d], m[d * C:(d + 1) * C] * s["decay"][d], 0.0)
             for s, m in zip(st, kq) for d in range(2)]
        t = _unit_tri_inverse(a, eye, ri, ci, C)
        eg = [jnp.exp(s["gc"][d]) for s in st for d in range(2)]
        sol = [_mm(t[2 * n + d], jnp.concatenate([s["vb"][d], s["kb"][d] * eg[2 * n + d]], axis=1))
               for n, s in enumerate(st) for d in range(2)]
        for n, (s, m) in enumerate(zip(st, kq)):
            for d in range(2):
                x, e = sol[2 * n + d], eg[2 * n + d]
                i, c_idx, rows = 2 * hh + d, s["c"], s["rows"]
                u_ref[i, rows, :] = x[:, :DN_HEAD_DIM]
                wq_ref[i, chunk_rows(c_idx, 2 * C), :] = jnp.concatenate(
                    [x[:, DN_HEAD_DIM:], s["q"] * e], axis=0).astype(BF16)
                qk_ref[i, rows, :] = (m[2 * C:] * s["decay"][d]).astype(BF16)
                kd_ref[i, rows, :] = (s["k"] * jnp.exp(s["g_last"][d] - s["gc"][d])).astype(BF16)
                gl_ref[i, chunk_rows(c_idx, V7X_SUBLANES), :] = jnp.broadcast_to(
                    jnp.exp(s["g_last"][d]), (V7X_SUBLANES, DN_HEAD_DIM))

    for hh in range(NH):
        def prepare_body(i, carry, hh=hh):
            prepare(hh, [i * DN_PREP_CHUNKS + j for j in range(DN_PREP_CHUNKS)])
            return carry

        lax.fori_loop(0, n_chunks // DN_PREP_CHUNKS, prepare_body, 0)

    chains = [(hh, d) for hh in range(NH) for d in range(2)]

    def chunk_body(c, states, *, accumulate):
        idx = (c, n_chunks - 1 - c)
        s16 = [s.astype(BF16) for s in states]
        ws = [_dot(wq_ref[2 * hh + d, chunk_rows(idx[d], 2 * C), :], s16[n])
              for n, (hh, d) in enumerate(chains)]
        v_new = [(u_ref[2 * hh + d, chunk_rows(idx[d], C), :] - ws[n][:C]).astype(BF16)
                 for n, (hh, d) in enumerate(chains)]
        for n, (hh, d) in enumerate(chains):
            o = ws[n][C:] + _dot(qk_ref[2 * hh + d, chunk_rows(idx[d], C), :], v_new[n])
            rows = chunk_rows(idx[d], C)
            cols = slice(hh * DN_HEAD_DIM, (hh + 1) * DN_HEAD_DIM)
            if accumulate:
                o_ref[0, rows, cols] += o
            else:
                o_ref[0, rows, cols] = o
        return tuple(
            states[n] * gl_ref[2 * hh + d, chunk_rows(idx[d], V7X_SUBLANES), :][0:1, :]
            + _dot_tn(kd_ref[2 * hh + d, chunk_rows(idx[d], C), :], v_new[n])
            for n, (hh, d) in enumerate(chains))

    assert n_chunks % 2 == 0
    zero_state = jnp.zeros((DN_HEAD_DIM, DN_HEAD_DIM), F32)
    states = lax.fori_loop(0, n_chunks // 2, functools.partial(chunk_body, accumulate=False),
                           (zero_state,) * len(chains))
    lax.fori_loop(n_chunks // 2, n_chunks, functools.partial(chunk_body, accumulate=True), states)


def _deltanet(qkv, bg):
    B, S, _ = qkv.shape
    C = DN_CHUNK
    hd = DN_HEAD_DIM
    NH = DN_HEADS_PER_STEP
    steps = DN_HEADS // NH
    seq = lambda off: pl.BlockSpec((1, S, NH * hd), lambda b, h: (b, 0, off + h))
    lanes = NH * 2
    return pl.pallas_call(
        functools.partial(_deltanet_kernel, S=S, C=C),
        grid=(B, steps),
        in_specs=[
            seq(0), seq(steps), seq(2 * steps),
            pl.BlockSpec((1, S, V7X_LANES), lambda b, h: (b, 0, 0)),
        ],
        out_specs=pl.BlockSpec((1, S, NH * hd), lambda b, h: (b, 0, h)),
        out_shape=jax.ShapeDtypeStruct((B, S, DN_WIDTH), F32),
        scratch_shapes=[
            pltpu.VMEM((lanes, S, hd), F32), pltpu.VMEM((lanes, S, hd), F32),
            pltpu.VMEM((lanes, S, hd), F32),
            pltpu.VMEM((lanes, 2 * S, hd), BF16),
            pltpu.VMEM((lanes, S, C), BF16),
            pltpu.VMEM((lanes, S, hd), BF16),
            pltpu.VMEM((lanes, (S // C) * V7X_SUBLANES, hd), F32),
        ],
        compiler_params=_params("parallel", "parallel"),
        name="deltanet",
    )(qkv, qkv, qkv, bg)


POOL_HALO = 2 * V7X_SUBLANES
POOL_TAIL = 4 * V7X_SUBLANES


def _window_sum(blk, win, n):
    lo_off = -(win // 2)
    if win <= 4:
        acc = blk[POOL_HALO + lo_off:POOL_HALO + lo_off + n, :]
        for off in range(lo_off + 1, win + lo_off):
            acc = acc + blk[POOL_HALO + off:POOL_HALO + off + n, :]
        return acc
    levels = win.bit_length() - 1
    spare = V7X_SUBLANES * levels
    p = blk[POOL_HALO + lo_off:POOL_HALO + lo_off + n + spare, :]
    span = 1
    while span < win:
        spare -= V7X_SUBLANES
        p = p[0:n + spare, :] + p[span:span + n + spare, :]
        span *= 2
    return p


def _pool_kernel(u_ref, w_ref, sc_ref, o_ref, pad_ref, *, S):
    grp = pl.program_id(1)
    RC = ROW_CHUNK
    HALO = POOL_HALO
    gd = POOL_GROUP_DIM
    pad_ref[0:HALO, :] = jnp.zeros((HALO, gd), F32)
    pad_ref[HALO + S:HALO + S + POOL_TAIL, :] = jnp.zeros((POOL_TAIL, gd), F32)
    pad_ref[HALO:HALO + S, :] = u_ref[0]
    w16 = w_ref[0].astype(BF16)
    scale = sc_ref[...]

    for gi, win in enumerate(POOL_WINDOWS):
        lo_off = -(win // 2)
        hi_off = win - win // 2

        @pl.when(grp == gi)
        def _():
            def body(i, carry):
                r0 = pl.multiple_of(i * RC, RC)
                blk = pad_ref[pl.ds(r0, RC + HALO + POOL_TAIL), :]
                acc = _window_sum(blk, win, RC)
                pos = r0 + lax.broadcasted_iota(jnp.int32, (RC, 1), 0)
                cnt = jnp.minimum(pos + hi_off, S) - jnp.maximum(pos + lo_off, 0)
                pooled = acc / cnt.astype(F32) - blk[HALO:HALO + RC, :]
                mixed = _dot(pooled.astype(BF16), w16) * scale
                o_ref[0, pl.ds(r0, RC), :] = mixed.astype(o_ref.dtype)
                return carry

            lax.fori_loop(0, S // RC, body, 0, unroll=2)


def _pool(u, pool_w, pool_scale):
    B, S, _ = u.shape
    gd = POOL_GROUP_DIM
    return pl.pallas_call(
        functools.partial(_pool_kernel, S=S),
        grid=(B, len(POOL_WINDOWS)),
        in_specs=[
            pl.BlockSpec((1, S, gd), lambda b, g: (b, 0, g)),
            pl.BlockSpec((1, gd, gd), lambda b, g: (g, 0, 0)),
            pl.BlockSpec((1, gd), lambda b, g: (0, g)),
        ],
        out_specs=pl.BlockSpec((1, S, gd), lambda b, g: (b, 0, g)),
        out_shape=jax.ShapeDtypeStruct((B, S, POOL_WIDTH), BF16),
        scratch_shapes=[pltpu.VMEM((S + POOL_HALO + POOL_TAIL, gd), F32)],
        compiler_params=_params("parallel", "parallel"),
        name="pool",
    )(u, pool_w, pool_scale)


def _dilated_kernel(q_ref, k_ref, v_ref, o_ref,
                    qd_ref, kd_ref, vd_ref, od_ref, ld_ref, on_ref, ln_ref, num_ref, den_ref, mx_ref, tmp_ref, *, S):
    grp = pl.program_id(2)
    QB = DA_QBLOCK
    W = V7X_LANES
    n_it = S // QB
    lane = lax.broadcasted_iota(jnp.int32, (1, W), 1)
    pair_mask = [((lane // (DA_HEAD_DIM // 2)) % 2 == e) for e in range(2)]
    first_head = lane < DA_HEAD_DIM
    qi2 = lax.broadcasted_iota(jnp.int32, (2 * QB, 2 * QB), 0) % QB
    kj2 = lax.broadcasted_iota(jnp.int32, (2 * QB, 2 * QB), 1)
    kcol = lax.broadcasted_iota(jnp.int32, (1, 2 * QB), 1)

    for gi, (window, dil) in enumerate(DA_CONFIGS):
        radius = window // (2 * dil)
        assert radius == QB // 2
        L = S // dil
        LP = L + QB
        blocks_per_res = L // QB
        band2 = jnp.abs(kj2 - radius - qi2) <= radius

        @pl.when(grp == gi)
        def _():
            zpad = jnp.zeros((radius, W), BF16)
            two_pass = dil > DA_MAX_STRIDE
            part = S // DA_MAX_STRIDE

            def class_rows(x_ref, r):
                if dil == 1:
                    return x_ref[0]
                if not two_pass:
                    return x_ref[0, pl.ds(r, L, stride=dil), :]
                r1, r2 = r % DA_MAX_STRIDE, r // DA_MAX_STRIDE
                return tmp_ref[pl.ds(r1 * part + r2, L, stride=dil // DA_MAX_STRIDE), :]

            def stage(x_ref):
                if two_pass:
                    for r1 in range(DA_MAX_STRIDE):
                        tmp_ref[r1 * part:(r1 + 1) * part, :] = x_ref[0, pl.ds(r1, part, stride=DA_MAX_STRIDE), :]

            stage(q_ref)
            for r in range(dil):
                qd_ref[r * L:(r + 1) * L, :] = class_rows(q_ref, r).astype(BF16)
            for x_ref, xd_ref in ((k_ref, kd_ref), (v_ref, vd_ref)):
                stage(x_ref)
                for r in range(dil):
                    base = r * LP
                    xd_ref[base:base + radius, :] = zpad
                    xd_ref[base + radius:base + radius + L, :] = class_rows(x_ref, r).astype(BF16)
                    xd_ref[base + radius + L:base + LP, :] = zpad

            def body(step, carry):
                blocks = []
                for j in range(DA_LOCKSTEP):
                    it = step * DA_LOCKSTEP + j
                    res = it // blocks_per_res
                    m0 = (it % blocks_per_res) * QB
                    q0 = pl.multiple_of(it * QB, QB)
                    k0 = pl.multiple_of(it * QB + res * QB, QB)
                    kpos = kcol + (m0 - radius)
                    qp = qd_ref[pl.ds(q0, QB), :]
                    zero = jnp.zeros_like(qp)
                    blocks.append(dict(
                        q0=q0, valid=band2 & (kpos >= 0) & (kpos < L),
                        q=jnp.concatenate([jnp.where(pair_mask[e], qp, zero) for e in range(2)], axis=0),
                        k=kd_ref[pl.ds(k0, 2 * QB), :], v=vd_ref[pl.ds(k0, 2 * QB), :]))
                scores = [jnp.where(b["valid"], _dot_nt(b["q"], b["k"]), MASK_VALUE) for b in blocks]
                mxs = [jnp.max(s, axis=-1, keepdims=True) for s in scores]
                probs = [jnp.exp(s - m) for s, m in zip(scores, mxs)]
                dens = [jnp.sum(p, axis=-1, keepdims=True) for p in probs]
                outs = [_dot(p.astype(BF16), b["v"]) / d for p, b, d in zip(probs, blocks, dens)]
                for b, o, m, d in zip(blocks, outs, mxs, dens):
                    lse = m + jnp.log(d)
                    od_ref[pl.ds(b["q0"], QB), :] = jnp.where(first_head, o[:QB], o[QB:])
                    ld_ref[pl.ds(b["q0"], QB), :] = jnp.where(first_head, lse[:QB], lse[QB:])
                return carry

            lax.fori_loop(0, n_it // DA_LOCKSTEP, body, 0)

            for xd_ref, xn_ref in ((od_ref, on_ref), (ld_ref, ln_ref)):
                for r in range(dil):
                    rows = xd_ref[r * L:(r + 1) * L, :]
                    if dil == 1:
                        xn_ref[...] = rows
                    elif not two_pass:
                        xn_ref[pl.ds(r, L, stride=dil), :] = rows
                    else:
                        r1, r2 = r % DA_MAX_STRIDE, r // DA_MAX_STRIDE
                        tmp_ref[pl.ds(r1 * part + r2, L, stride=dil // DA_MAX_STRIDE), :] = rows
                if two_pass:
                    for r1 in range(DA_MAX_STRIDE):
                        xn_ref[pl.ds(r1, part, stride=DA_MAX_STRIDE), :] = tmp_ref[r1 * part:(r1 + 1) * part, :]

    @pl.when(grp == 0)
    def _():
        num_ref[...] = on_ref[...]
        den_ref[...] = jnp.ones_like(den_ref)
        mx_ref[...] = ln_ref[...]

    @pl.when(grp > 0)
    def _():
        m_old = mx_ref[...]
        lse = ln_ref[...]
        m_new = jnp.maximum(m_old, lse)
        a_old = jnp.exp(m_old - m_new)
        a_new = jnp.exp(lse - m_new)
        num_ref[...] = num_ref[...] * a_old + on_ref[...] * a_new
        den_ref[...] = den_ref[...] * a_old + a_new
        mx_ref[...] = m_new

    @pl.when(grp == DA_NGROUPS - 1)
    def _():
        o_ref[0] = (num_ref[...] / den_ref[...]).astype(o_ref.dtype)


def _dilated(daq, dak, dav):
    B, S, _ = daq.shape
    W = V7X_LANES
    pairs = DA_GROUP_WIDTH // W
    max_dil = max(d for _, d in DA_CONFIGS)
    grp_spec = pl.BlockSpec((1, S, W), lambda b, p, g: (b, 0, g * pairs + p))
    full = lambda dt: pltpu.VMEM((S, W), dt)
    return pl.pallas_call(
        functools.partial(_dilated_kernel, S=S),
        grid=(B, pairs, DA_NGROUPS),
        in_specs=[grp_spec, grp_spec, grp_spec],
        out_specs=pl.BlockSpec((1, S, W), lambda b, p, g: (b, 0, p)),
        out_shape=jax.ShapeDtypeStruct((B, S, DA_GROUP_WIDTH), BF16),
        scratch_shapes=[
            full(BF16),
            pltpu.VMEM((S + max_dil * DA_QBLOCK, W), BF16),
            pltpu.VMEM((S + max_dil * DA_QBLOCK, W), BF16),
            full(F32), full(F32), full(F32), full(F32), full(F32), full(F32), full(F32),
            full(F32),
        ],
        compiler_params=_params("parallel", "parallel", "arbitrary"),
        name="dilated",
    )(daq, dak, dav)


def _mix_out_kernel(x_ref, gain_ref, oa_ref, z_ref, onorm_ref, yb_ref, yc_ref, wg_ref, bg_ref, wa_ref, wb_ref,
                    wc_ref, wo_ref, o_ref):
    x = x_ref[...]
    D = x.shape[-1]
    h = _rms(x, gain_ref[...]).astype(BF16)
    o = oa_ref[...]
    z = z_ref[...]
    heads = []
    for hd in range(DN_HEADS):
        cols = slice(hd * DN_HEAD_DIM, (hd + 1) * DN_HEAD_DIM)
        zh = z[:, cols]
        heads.append((_rms(o[:, cols], onorm_ref[...]) * _silu(zh)).astype(BF16))
    ya = jnp.concatenate(heads, axis=1)
    merged = jnp.zeros(x.shape, F32)
    for i, (y, w_ref) in enumerate(((ya, wa_ref), (yb_ref[...], wb_ref), (yc_ref[...], wc_ref))):
        gate = _sigmoid(_dot(h, wg_ref[:, i * D:(i + 1) * D]) + bg_ref[:, i * D:(i + 1) * D])
        merged = merged + gate * _dot(y, w_ref[...])
    o_ref[...] = x + _dot(merged.astype(BF16), wo_ref[...])


def _mix_out(x, gain, oa, z, onorm, yb, yc, wg, bg, wa, wb, wc, wo):
    T, D = x.shape
    tm = MIX_TOKEN_TILE
    tok = lambda n: pl.BlockSpec((tm, n), lambda i: (i, 0))
    return pl.pallas_call(
        _mix_out_kernel,
        grid=(T // tm,),
        in_specs=[
            tok(D), _resident((1, D)),
            tok(oa.shape[1]), tok(z.shape[1]), _resident(onorm.shape), tok(yb.shape[1]), tok(yc.shape[1]),
            _resident(wg.shape), _resident(bg.shape), _resident(wa.shape), _resident(wb.shape),
            _resident(wc.shape), _resident(wo.shape),
        ],
        out_specs=tok(D),
        out_shape=jax.ShapeDtypeStruct((T, D), F32),
        compiler_params=_params("parallel"),
        name="mix_out",
    )(x, gain, oa, z, onorm, yb, yc, wg, bg, wa, wb, wc, wo)


def _pair_layout_columns():
    half = DA_HEAD_DIM // 2
    order = []
    for g in range(DA_NGROUPS):
        for p in range(DA_HEADS_PER_GROUP // 2):
            for part in range(2):
                for e in range(2):
                    head = 2 * p + e
                    start = g * DA_GROUP_WIDTH + head * DA_HEAD_DIM + part * half
                    order.extend(range(start, start + half))
    return jnp.asarray(order, dtype=jnp.int32)


def _rope_tables(seq_len):
    half = DA_HEAD_DIM // 2
    inv_freq = ROPE_THETA ** (-jnp.arange(half, dtype=F32) / half)
    ang = jnp.arange(seq_len).astype(F32)[:, None] * inv_freq[None, :]
    cos = jnp.tile(jnp.cos(ang), (1, V7X_LANES // half))
    sin = jnp.sin(ang)
    sin = jnp.concatenate([-sin, -sin, sin, sin], axis=1)
    return cos, sin


def kernel(x, ffn1_norm, ffn1_w_gate, ffn1_w_up, ffn1_w_down, mix_norm, w_in, dn_conv, dn_a_log, dn_dt_bias, dn_out_norm, pool_w, pool_scale, w_proj_a, w_proj_b, w_proj_c, w_gate, b_gate, w_out, ffn2_norm, ffn2_w_gate, ffn2_w_up, ffn2_w_down, final_norm):
    B, S, D = x.shape
    depth = w_in.shape[0]
    T = B * S
    off_z = 3 * DN_WIDTH
    off_beta = off_z + DN_WIDTH
    off_pool = off_beta + 4 * DN_HEADS
    off_da = off_pool + POOL_WIDTH
    pair_cols = _pair_layout_columns()
    cos, sin = _rope_tables(S)
    fgain = final_norm.reshape(1, D)

    xt = x.reshape(T, D)
    for l in range(depth):
        xt = _ffn(xt, ffn1_norm[l].reshape(1, D), ffn1_w_gate[l].astype(BF16), ffn1_w_up[l].astype(BF16),
                  ffn1_w_down[l].astype(BF16), fgain, final_norm=False)

        w = w_in[l]
        w_da = w[:, off_da:]
        w_da = jnp.concatenate([w_da[:, :DA_WIDTH][:, pair_cols],
                                w_da[:, DA_WIDTH:2 * DA_WIDTH][:, pair_cols],
                                w_da[:, 2 * DA_WIDTH:]], axis=1)
        w_ba = jnp.pad(w[:, off_beta:off_pool], ((0, 0), (0, V7X_LANES - 4 * DN_HEADS)))
        qkv, z, bg, u, daq, dak, dav = _mix_in(
            xt, mix_norm[l].reshape(1, D), w[:, :off_z].astype(BF16), w[:, off_z:off_beta].astype(BF16),
            w_ba.astype(BF16), w[:, off_pool:off_da].astype(BF16), w_da.astype(BF16), dn_conv[l],
            dn_a_log[l], dn_dt_bias[l], cos, sin, S)

        o_a = _deltanet(qkv.reshape(B, S, -1), bg.reshape(B, S, -1))
        y_b = _pool(u.reshape(B, S, -1), pool_w[l], pool_scale[l].reshape(1, POOL_WIDTH))
        y_c = _dilated(daq.reshape(B, S, -1), dak.reshape(B, S, -1), dav.reshape(B, S, -1))

        xt = _mix_out(xt, mix_norm[l].reshape(1, D), o_a.reshape(T, -1), z,
                      dn_out_norm[l].reshape(1, DN_HEAD_DIM), y_b.reshape(T, -1),
                      y_c.reshape(T, -1), w_gate[l].astype(BF16), b_gate[l].reshape(1, -1),
                      w_proj_a[l].astype(BF16), w_proj_b[l].astype(BF16), w_proj_c[l].astype(BF16),
                      w_out[l].astype(BF16))

        xt = _ffn(xt, ffn2_norm[l].reshape(1, D), ffn2_w_gate[l].astype(BF16), ffn2_w_up[l].astype(BF16),
                  ffn2_w_down[l].astype(BF16), fgain, final_norm=(l == depth - 1))
    return xt.reshape(B, S, D)
```

```python
import functools

import jax
import jax.numpy as jnp
from jax import lax
from jax.experimental import pallas as pl
from jax.experimental.pallas import tpu as pltpu

F32 = jnp.float32
BF16 = jnp.bfloat16

RMS_EPS = 1e-6
L2_EPS = 1e-6

DN_HEADS = 4
DN_HEAD_DIM = 128
DN_WIDTH = DN_HEADS * DN_HEAD_DIM
DN_CONV = 5
DN_CHUNK = 128
DN_PREP_CHUNKS = 8
DN_HEADS_PER_STEP = 2

POOL_WINDOWS = (2, 4, 8, 16)
POOL_GROUP_DIM = 128
POOL_WIDTH = len(POOL_WINDOWS) * POOL_GROUP_DIM

DA_CONFIGS = ((128, 1), (512, 4), (2048, 16))
DA_NGROUPS = len(DA_CONFIGS)
DA_HEADS_PER_GROUP = 4
DA_HEAD_DIM = 64
DA_GROUP_WIDTH = DA_HEADS_PER_GROUP * DA_HEAD_DIM
DA_WIDTH = DA_NGROUPS * DA_GROUP_WIDTH
DA_QBLOCK = 128
DA_LOCKSTEP = 8
DA_MAX_STRIDE = 4
ROPE_THETA = 10000.0
MASK_VALUE = -1e30

N_BRANCHES = 3

V7X_LANES = 128
V7X_SUBLANES = 8
V7X_VMEM_LIMIT_BYTES = 56 * 1024 * 1024

V7X_MXU_WIDTH = 256

FFN_TOKEN_TILE = 512
MIX_TOKEN_TILE = 512
ROW_CHUNK = 256


def _params(*semantics):
    return pltpu.CompilerParams(dimension_semantics=semantics,
                                vmem_limit_bytes=V7X_VMEM_LIMIT_BYTES)


def _resident(shape):
    nd = len(shape)
    return pl.BlockSpec(shape, lambda *_: (0,) * nd, pipeline_mode=pl.Buffered(1))


def _rms(x, gain):
    return x * lax.rsqrt(jnp.mean(x * x, axis=-1, keepdims=True) + RMS_EPS) * gain


def _sigmoid(x):
    return 0.5 * jnp.tanh(0.5 * x) + 0.5


def _silu(x):
    h = 0.5 * x
    return h + h * jnp.tanh(h)


def _dot(a, b):
    return jnp.dot(a, b, preferred_element_type=F32)


def _dot_nt(a, b):
    return lax.dot_general(a, b, (((1,), (1,)), ((), ())), preferred_element_type=F32)


def _dot_tn(a, b):
    return lax.dot_general(a, b, (((0,), (0,)), ((), ())), preferred_element_type=F32)


def _mm(a, b):
    return _dot(a.astype(BF16), b.astype(BF16))


def _ffn_kernel(x_ref, gain_ref, wg_ref, wu_ref, wd_ref, fgain_ref, o_ref, *, final_norm):
    x = x_ref[...]
    h = _rms(x, gain_ref[...]).astype(BF16)
    F = wg_ref.shape[1]
    acc = jnp.zeros(x.shape, F32)
    split = (F // (2 * V7X_MXU_WIDTH)) * V7X_MXU_WIDTH
    for cols in (slice(0, split), slice(split, F)):
        g = _dot(h, wg_ref[:, cols])
        u = _dot(h, wu_ref[:, cols])
        a = (_silu(g) * u).astype(BF16)
        acc = acc + _dot(a, wd_ref[cols, :])
    y = x + 0.5 * acc
    if final_norm:
        y = _rms(y, fgain_ref[...])
    o_ref[...] = y


def _ffn(x, gain, wg, wu, wd, fgain, *, final_norm):
    T, D = x.shape
    tm = FFN_TOKEN_TILE
    return pl.pallas_call(
        functools.partial(_ffn_kernel, final_norm=final_norm),
        grid=(T // tm,),
        in_specs=[
            pl.BlockSpec((tm, D), lambda i: (i, 0)),
            _resident((1, D)),
            _resident(wg.shape), _resident(wu.shape), _resident(wd.shape),
            _resident((1, D)),
        ],
        out_specs=pl.BlockSpec((tm, D), lambda i: (i, 0)),
        out_shape=jax.ShapeDtypeStruct((T, D), F32),
        compiler_params=_params("parallel"),
        name="ffn",
    )(x, gain, wg, wu, wd, fgain)


def _mix_in_kernel(x_ref, xp_ref, xn_ref, gain_ref, wqkv_ref, wz_ref, wba_ref, wpool_ref, wda_ref,
                   conv_ref, alog_ref, dtb_ref, cos_ref, sin_ref,
                   qkv_ref, z_ref, bg_ref, u_ref, daq_ref, dak_ref, dav_ref, *, tiles_per_seq):
    tm = x_ref.shape[0]
    HALO = V7X_SUBLANES
    half = DN_CONV // 2
    tile = pl.program_id(0) % tiles_per_seq
    gain = gain_ref[...]
    h = _rms(x_ref[...], gain).astype(BF16)

    h_prev = _rms(xp_ref[...], gain).astype(BF16)
    h_next = _rms(xn_ref[...], gain).astype(BF16)
    proj = _dot(jnp.concatenate([h_prev, h, h_next], axis=0), wqkv_ref[...])
    prev = jnp.where(tile > 0, proj[:HALO], 0.0)
    nxt = jnp.where(tile < tiles_per_seq - 1, proj[HALO + tm:], 0.0)
    win = jnp.concatenate([prev, proj[HALO:HALO + tm], nxt], axis=0)
    cw = conv_ref[...]
    for cb in range(3 * DN_HEADS):
        cols = slice(cb * DN_HEAD_DIM, (cb + 1) * DN_HEAD_DIM)
        wc = win[:, cols]
        acc = jnp.zeros((tm, DN_HEAD_DIM), F32)
        for j in range(DN_CONV):
            tap = wc if j == half else pltpu.roll(wc, (half - j) % (tm + 2 * HALO), 0)
            acc = acc + tap[HALO:HALO + tm, :] * cw[j:j + 1, cols]
        y = _silu(acc)
        if cb < 2 * DN_HEADS:
            inv = lax.rsqrt(jnp.sum(y * y, axis=-1, keepdims=True) + L2_EPS)
            y = y * (inv * DN_HEAD_DIM ** -0.5 if cb < DN_HEADS else inv)
        qkv_ref[:, cols] = y

    z_ref[...] = _dot(h, wz_ref[...])

    ba = _dot(h, wba_ref[...])
    lane = lax.broadcasted_iota(jnp.int32, (1, V7X_LANES), 1)
    a_raw = ba + dtb_ref[...]
    softplus = jnp.maximum(a_raw, 0.0) + jnp.log(1.0 + jnp.exp(-jnp.abs(a_raw)))
    bg_ref[...] = jnp.where(lane < 2 * DN_HEADS, _sigmoid(ba), -jnp.exp(alog_ref[...]) * softplus)

    u_ref[...] = _dot(h, wpool_ref[...])
    da = _dot(h, wda_ref[...])
    cos = cos_ref[...]
    sin = sin_ref[...]
    for part, out_ref, scale in ((0, daq_ref, DA_HEAD_DIM ** -0.5), (1, dak_ref, 1.0)):
        for cb in range(DA_WIDTH // V7X_LANES):
            lo = part * DA_WIDTH + cb * V7X_LANES
            t = da[:, lo:lo + V7X_LANES]
            r = t * cos + pltpu.roll(t, V7X_LANES // 2, 1) * sin
            out_ref[:, cb * V7X_LANES:(cb + 1) * V7X_LANES] = r if scale == 1.0 else r * scale
    dav_ref[...] = da[:, 2 * DA_WIDTH:]


def _lane_row(p):
    return jnp.pad(p.reshape(1, -1), ((0, 0), (2 * DN_HEADS, V7X_LANES - 4 * DN_HEADS)))


def _mix_in(x, gain, wqkv, wz, wba, wpool, wda, conv_w, a_log, dt_bias, cos, sin, seq_len):
    T, D = x.shape
    tm = MIX_TOKEN_TILE
    tiles_per_seq = seq_len // tm
    hb = tm // V7X_SUBLANES
    tok = lambda n: pl.BlockSpec((tm, n), lambda i: (i, 0))
    return pl.pallas_call(
        functools.partial(_mix_in_kernel, tiles_per_seq=tiles_per_seq),
        grid=(T // tm,),
        in_specs=[
            tok(D),
            pl.BlockSpec((V7X_SUBLANES, D), lambda i: (jnp.maximum(i * hb - 1, 0), 0)),
            pl.BlockSpec((V7X_SUBLANES, D), lambda i: (jnp.minimum((i + 1) * hb, T // V7X_SUBLANES - 1), 0)),
            _resident((1, D)),
            _resident(wqkv.shape), _resident(wz.shape), _resident(wba.shape),
            _resident(wpool.shape), _resident(wda.shape),
            _resident(conv_w.shape), _resident((1, V7X_LANES)), _resident((1, V7X_LANES)),
            pl.BlockSpec((tm, V7X_LANES), lambda i: (i % tiles_per_seq, 0)),
            pl.BlockSpec((tm, V7X_LANES), lambda i: (i % tiles_per_seq, 0)),
        ],
        out_specs=[tok(3 * DN_WIDTH), tok(DN_WIDTH), tok(V7X_LANES), tok(POOL_WIDTH),
                   tok(DA_WIDTH), tok(DA_WIDTH), tok(DA_WIDTH)],
        out_shape=[
            jax.ShapeDtypeStruct((T, 3 * DN_WIDTH), F32),
            jax.ShapeDtypeStruct((T, DN_WIDTH), F32),
            jax.ShapeDtypeStruct((T, V7X_LANES), F32),
            jax.ShapeDtypeStruct((T, POOL_WIDTH), F32),
            jax.ShapeDtypeStruct((T, DA_WIDTH), F32),
            jax.ShapeDtypeStruct((T, DA_WIDTH), F32),
            jax.ShapeDtypeStruct((T, DA_WIDTH), F32),
        ],
        compiler_params=_params("parallel"),
        name="mix_in",
    )(x, x, x, gain, wqkv, wz, wba, wpool, wda, conv_w, _lane_row(a_log), _lane_row(dt_bias), cos, sin)


def _unit_tri_inverse(mats, eye, ri, ci, n):
    blk = 2
    same = (ri // blk) == (ci // blk)
    ts = [eye - jnp.where(same, a, 0.0) for a in mats]
    while blk < n:
        joins = ((ri // (2 * blk)) == (ci // (2 * blk))) & ((ri // blk) != (ci // blk))
        ets = [_mm(jnp.where(joins, a, 0.0), t) for a, t in zip(mats, ts)]
        ts = [t - _mm(t, et) for t, et in zip(ts, ets)]
        blk *= 2
    return ts


def _deltanet_kernel(q_ref, k_ref, v_ref, bg_ref, o_ref,
                     beta_ref, gc_ref, u_ref, wq_ref, qk_ref, kd_ref, gl_ref, *, S, C):
    assert C == V7X_LANES
    RC = ROW_CHUNK
    n_rc = S // RC
    NH = DN_HEADS_PER_STEP

    lane = lax.broadcasted_iota(jnp.int32, (1, V7X_LANES), 1)
    rr = lax.broadcasted_iota(jnp.int32, (RC, RC), 0)
    rc = lax.broadcasted_iota(jnp.int32, (RC, RC), 1)
    same_chunk = (rr // C) == (rc // C)
    cum_mat = [(same_chunk & (rr >= rc)).astype(BF16), (same_chunk & (rr <= rc)).astype(BF16)]
    GRP = 4 * DN_HEADS
    sel_row = lax.broadcasted_iota(jnp.int32, (V7X_LANES, V7X_LANES), 0)

    def gate_body(i, carry):
        r0 = pl.multiple_of(i * RC, RC)
        bg = bg_ref[0, pl.ds(r0, RC), :]
        g_hi = bg.astype(BF16).astype(F32)
        g_mid = (bg - g_hi).astype(BF16).astype(F32)
        g_lo = (bg - g_hi) - g_mid
        parts = jnp.where(lane < GRP, g_hi,
                          jnp.where(lane < 2 * GRP, pltpu.roll(g_mid, GRP, 1),
                                    jnp.where(lane < 3 * GRP, pltpu.roll(g_lo, 2 * GRP, 1), 0.0))).astype(BF16)

        def pick(x, sel):
            col = jnp.sum(jnp.where(sel, x, 0.0), axis=-1, keepdims=True)
            return jnp.broadcast_to(col, (RC, DN_HEAD_DIM))

        for d in range(2):
            cum = _dot(cum_mat[d], parts)
            for hh in range(NH):
                head = pl.program_id(1) * NH + hh
                cb = d * DN_HEADS + head
                beta_sel = ((sel_row == cb) | (sel_row == cb + GRP) | (sel_row == cb + 2 * GRP)).astype(BF16)
                beta_ref[2 * hh + d, pl.ds(r0, RC), :] = _dot(parts, beta_sel)
                c = 2 * DN_HEADS + d * DN_HEADS + head
                gc_ref[2 * hh + d, pl.ds(r0, RC), :] = pick(
                    cum, (lane == c) | (lane == c + GRP) | (lane == c + 2 * GRP))
        return carry

    lax.fori_loop(0, n_rc, gate_body, 0, unroll=2)

    ri = lax.broadcasted_iota(jnp.int32, (C, C), 0)
    ci = lax.broadcasted_iota(jnp.int32, (C, C), 1)
    eye = (ri == ci).astype(F32)
    incl = [ri >= ci, ri <= ci]
    strict = [ri > ci, ri < ci]
    n_chunks = S // C

    def chunk_rows(c_idx, n):
        return pl.ds(pl.multiple_of(c_idx * n, n), n)

    def prepare(hh, chunks):
        cols = slice(hh * DN_HEAD_DIM, (hh + 1) * DN_HEAD_DIM)
        st = []
        for c_idx in chunks:
            rows = chunk_rows(c_idx, C)
            k = k_ref[0, rows, cols]
            v = v_ref[0, rows, cols]
            beta = [beta_ref[2 * hh + d, rows, :] for d in range(2)]
            gc = [gc_ref[2 * hh + d, rows, :] for d in range(2)]
            st.append(dict(
                c=c_idx, rows=rows, q=q_ref[0, rows, cols], k=k, gc=gc,
                kb=[k * b for b in beta], vb=[v * b for b in beta],
                decay=[jnp.where(incl[d], jnp.exp(gc[d] - gc[d].T), 0.0) for d in range(2)],
                g_last=[gc[0][C - 1:C, :], gc[1][0:1, :]]))
        kq = [_dot_nt(jnp.concatenate(s["kb"] + [s["q"]], axis=0).astype(BF16), s["k"].astype(BF16)) for s in st]
        a = [jnp.where(strict[d], m[d * C:(d + 1) * C] * s["decay"][d], 0.0)
             for s, m in zip(st, kq) for d in range(2)]
        t = _unit_tri_inverse(a, eye, ri, ci, C)
        eg = [jnp.exp(s["gc"][d]) for s in st for d in range(2)]
        sol = [_mm(t[2 * n + d], jnp.concatenate([s["vb"][d], s["kb"][d] * eg[2 * n + d]], axis=1))
               for n, s in enumerate(st) for d in range(2)]
        for n, (s, m) in enumerate(zip(st, kq)):
            for d in range(2):
                x, e = sol[2 * n + d], eg[2 * n + d]
                i, c_idx, rows = 2 * hh + d, s["c"], s["rows"]
                u_ref[i, rows, :] = x[:, :DN_HEAD_DIM]
                wq_ref[i, chunk_rows(c_idx, 2 * C), :] = jnp.concatenate(
                    [x[:, DN_HEAD_DIM:], s["q"] * e], axis=0).astype(BF16)
                qk_ref[i, rows, :] = (m[2 * C:] * s["decay"][d]).astype(BF16)
                kd_ref[i, rows, :] = (s["k"] * jnp.exp(s["g_last"][d] - s["gc"][d])).astype(BF16)
                gl_ref[i, chunk_rows(c_idx, V7X_SUBLANES), :] = jnp.broadcast_to(
                    jnp.exp(s["g_last"][d]), (V7X_SUBLANES, DN_HEAD_DIM))

    for hh in range(NH):
        def prepare_body(i, carry, hh=hh):
            prepare(hh, [i * DN_PREP_CHUNKS + j for j in range(DN_PREP_CHUNKS)])
            return carry

        lax.fori_loop(0, n_chunks // DN_PREP_CHUNKS, prepare_body, 0)

    chains = [(hh, d) for hh in range(NH) for d in range(2)]

    def chunk_body(c, states, *, accumulate):
        idx = (c, n_chunks - 1 - c)
        s16 = [s.astype(BF16) for s in states]
        ws = [_dot(wq_ref[2 * hh + d, chunk_rows(idx[d], 2 * C), :], s16[n])
              for n, (hh, d) in enumerate(chains)]
        v_new = [(u_ref[2 * hh + d, chunk_rows(idx[d], C), :] - ws[n][:C]).astype(BF16)
                 for n, (hh, d) in enumerate(chains)]
        for n, (hh, d) in enumerate(chains):
            o = ws[n][C:] + _dot(qk_ref[2 * hh + d, chunk_rows(idx[d], C), :], v_new[n])
            rows = chunk_rows(idx[d], C)
            cols = slice(hh * DN_HEAD_DIM, (hh + 1) * DN_HEAD_DIM)
            if accumulate:
                o_ref[0, rows, cols] += o
            else:
                o_ref[0, rows, cols] = o
        return tuple(
            states[n] * gl_ref[2 * hh + d, chunk_rows(idx[d], V7X_SUBLANES), :][0:1, :]
            + _dot_tn(kd_ref[2 * hh + d, chunk_rows(idx[d], C), :], v_new[n])
            for n, (hh, d) in enumerate(chains))

    assert n_chunks % 2 == 0
    zero_state = jnp.zeros((DN_HEAD_DIM, DN_HEAD_DIM), F32)
    states = lax.fori_loop(0, n_chunks // 2, functools.partial(chunk_body, accumulate=False),
                           (zero_state,) * len(chains))
    lax.fori_loop(n_chunks // 2, n_chunks, functools.partial(chunk_body, accumulate=True), states)


def _deltanet(qkv, bg):
    B, S, _ = qkv.shape
    C = DN_CHUNK
    hd = DN_HEAD_DIM
    NH = DN_HEADS_PER_STEP
    steps = DN_HEADS // NH
    seq = lambda off: pl.BlockSpec((1, S, NH * hd), lambda b, h: (b, 0, off + h))
    lanes = NH * 2
    return pl.pallas_call(
        functools.partial(_deltanet_kernel, S=S, C=C),
        grid=(B, steps),
        in_specs=[
            seq(0), seq(steps), seq(2 * steps),
            pl.BlockSpec((1, S, V7X_LANES), lambda b, h: (b, 0, 0)),
        ],
        out_specs=pl.BlockSpec((1, S, NH * hd), lambda b, h: (b, 0, h)),
        out_shape=jax.ShapeDtypeStruct((B, S, DN_WIDTH), F32),
        scratch_shapes=[
            pltpu.VMEM((lanes, S, hd), F32), pltpu.VMEM((lanes, S, hd), F32),
            pltpu.VMEM((lanes, S, hd), F32),
            pltpu.VMEM((lanes, 2 * S, hd), BF16),
            pltpu.VMEM((lanes, S, C), BF16),
            pltpu.VMEM((lanes, S, hd), BF16),
            pltpu.VMEM((lanes, (S // C) * V7X_SUBLANES, hd), F32),
        ],
        compiler_params=_params("parallel", "parallel"),
        name="deltanet",
    )(qkv, qkv, qkv, bg)


POOL_HALO = 2 * V7X_SUBLANES
POOL_TAIL = 4 * V7X_SUBLANES


def _window_sum(blk, win, n):
    lo_off = -(win // 2)
    if win <= 4:
        acc = blk[POOL_HALO + lo_off:POOL_HALO + lo_off + n, :]
        for off in range(lo_off + 1, win + lo_off):
            acc = acc + blk[POOL_HALO + off:POOL_HALO + off + n, :]
        return acc
    levels = win.bit_length() - 1
    spare = V7X_SUBLANES * levels
    p = blk[POOL_HALO + lo_off:POOL_HALO + lo_off + n + spare, :]
    span = 1
    while span < win:
        spare -= V7X_SUBLANES
        p = p[0:n + spare, :] + p[span:span + n + spare, :]
        span *= 2
    return p


def _pool_kernel(u_ref, w_ref, sc_ref, o_ref, pad_ref, *, S):
    grp = pl.program_id(1)
    RC = ROW_CHUNK
    HALO = POOL_HALO
    gd = POOL_GROUP_DIM
    pad_ref[0:HALO, :] = jnp.zeros((HALO, gd), F32)
    pad_ref[HALO + S:HALO + S + POOL_TAIL, :] = jnp.zeros((POOL_TAIL, gd), F32)
    pad_ref[HALO:HALO + S, :] = u_ref[0]
    w16 = w_ref[0].astype(BF16)
    scale = sc_ref[...]

    for gi, win in enumerate(POOL_WINDOWS):
        lo_off = -(win // 2)
        hi_off = win - win // 2

        @pl.when(grp == gi)
        def _():
            def body(i, carry):
                r0 = pl.multiple_of(i * RC, RC)
                blk = pad_ref[pl.ds(r0, RC + HALO + POOL_TAIL), :]
                acc = _window_sum(blk, win, RC)
                pos = r0 + lax.broadcasted_iota(jnp.int32, (RC, 1), 0)
                cnt = jnp.minimum(pos + hi_off, S) - jnp.maximum(pos + lo_off, 0)
                pooled = acc / cnt.astype(F32) - blk[HALO:HALO + RC, :]
                mixed = _dot(pooled.astype(BF16), w16) * scale
                o_ref[0, pl.ds(r0, RC), :] = mixed.astype(o_ref.dtype)
                return carry

            lax.fori_loop(0, S // RC, body, 0, unroll=2)


def _pool(u, pool_w, pool_scale):
    B, S, _ = u.shape
    gd = POOL_GROUP_DIM
    return pl.pallas_call(
        functools.partial(_pool_kernel, S=S),
        grid=(B, len(POOL_WINDOWS)),
        in_specs=[
            pl.BlockSpec((1, S, gd), lambda b, g: (b, 0, g)),
            pl.BlockSpec((1, gd, gd), lambda b, g: (g, 0, 0)),
            pl.BlockSpec((1, gd), lambda b, g: (0, g)),
        ],
        out_specs=pl.BlockSpec((1, S, gd), lambda b, g: (b, 0, g)),
        out_shape=jax.ShapeDtypeStruct((B, S, POOL_WIDTH), BF16),
        scratch_shapes=[pltpu.VMEM((S + POOL_HALO + POOL_TAIL, gd), F32)],
        compiler_params=_params("parallel", "parallel"),
        name="pool",
    )(u, pool_w, pool_scale)


def _dilated_kernel(q_ref, k_ref, v_ref, o_ref,
                    qd_ref, kd_ref, vd_ref, od_ref, ld_ref, on_ref, ln_ref, num_ref, den_ref, mx_ref, tmp_ref, *, S):
    grp = pl.program_id(2)
    QB = DA_QBLOCK
    W = V7X_LANES
    n_it = S // QB
    lane = lax.broadcasted_iota(jnp.int32, (1, W), 1)
    pair_mask = [((lane // (DA_HEAD_DIM // 2)) % 2 == e) for e in range(2)]
    first_head = lane < DA_HEAD_DIM
    qi2 = lax.broadcasted_iota(jnp.int32, (2 * QB, 2 * QB), 0) % QB
    kj2 = lax.broadcasted_iota(jnp.int32, (2 * QB, 2 * QB), 1)
    kcol = lax.broadcasted_iota(jnp.int32, (1, 2 * QB), 1)

    for gi, (window, dil) in enumerate(DA_CONFIGS):
        radius = window // (2 * dil)
        assert radius == QB // 2
        L = S // dil
        LP = L + QB
        blocks_per_res = L // QB
        band2 = jnp.abs(kj2 - radius - qi2) <= radius

        @pl.when(grp == gi)
        def _():
            zpad = jnp.zeros((radius, W), BF16)
            two_pass = dil > DA_MAX_STRIDE
            part = S // DA_MAX_STRIDE

            def class_rows(x_ref, r):
                if dil == 1:
                    return x_ref[0]
                if not two_pass:
                    return x_ref[0, pl.ds(r, L, stride=dil), :]
                r1, r2 = r % DA_MAX_STRIDE, r // DA_MAX_STRIDE
                return tmp_ref[pl.ds(r1 * part + r2, L, stride=dil // DA_MAX_STRIDE), :]

            def stage(x_ref):
                if two_pass:
                    for r1 in range(DA_MAX_STRIDE):
                        tmp_ref[r1 * part:(r1 + 1) * part, :] = x_ref[0, pl.ds(r1, part, stride=DA_MAX_STRIDE), :]

            stage(q_ref)
            for r in range(dil):
                qd_ref[r * L:(r + 1) * L, :] = class_rows(q_ref, r).astype(BF16)
            for x_ref, xd_ref in ((k_ref, kd_ref), (v_ref, vd_ref)):
                stage(x_ref)
                for r in range(dil):
                    base = r * LP
                    xd_ref[base:base + radius, :] = zpad
                    xd_ref[base + radius:base + radius + L, :] = class_rows(x_ref, r).astype(BF16)
                    xd_ref[base + radius + L:base + LP, :] = zpad

            def body(step, carry):
                blocks = []
                for j in range(DA_LOCKSTEP):
                    it = step * DA_LOCKSTEP + j
                    res = it // blocks_per_res
                    m0 = (it % blocks_per_res) * QB
                    q0 = pl.multiple_of(it * QB, QB)
                    k0 = pl.multiple_of(it * QB + res * QB, QB)
                    kpos = kcol + (m0 - radius)
                    qp = qd_ref[pl.ds(q0, QB), :]
                    zero = jnp.zeros_like(qp)
                    blocks.append(dict(
                        q0=q0, valid=band2 & (kpos >= 0) & (kpos < L),
                        q=jnp.concatenate([jnp.where(pair_mask[e], qp, zero) for e in range(2)], axis=0),
                        k=kd_ref[pl.ds(k0, 2 * QB), :], v=vd_ref[pl.ds(k0, 2 * QB), :]))
                scores = [jnp.where(b["valid"], _dot_nt(b["q"], b["k"]), MASK_VALUE) for b in blocks]
                mxs = [jnp.max(s, axis=-1, keepdims=True) for s in scores]
                probs = [jnp.exp(s - m) for s, m in zip(scores, mxs)]
                dens = [jnp.sum(p, axis=-1, keepdims=True) for p in probs]
                outs = [_dot(p.astype(BF16), b["v"]) / d for p, b, d in zip(probs, blocks, dens)]
                for b, o, m, d in zip(blocks, outs, mxs, dens):
                    lse = m + jnp.log(d)
                    od_ref[pl.ds(b["q0"], QB), :] = jnp.where(first_head, o[:QB], o[QB:])
                    ld_ref[pl.ds(b["q0"], QB), :] = jnp.where(first_head, lse[:QB], lse[QB:])
                return carry

            lax.fori_loop(0, n_it // DA_LOCKSTEP, body, 0)

            for xd_ref, xn_ref in ((od_ref, on_ref), (ld_ref, ln_ref)):
                for r in range(dil):
                    rows = xd_ref[r * L:(r + 1) * L, :]
                    if dil == 1:
                        xn_ref[...] = rows
                    elif not two_pass:
                        xn_ref[pl.ds(r, L, stride=dil), :] = rows
                    else:
                        r1, r2 = r % DA_MAX_STRIDE, r // DA_MAX_STRIDE
                        tmp_ref[pl.ds(r1 * part + r2, L, stride=dil // DA_MAX_STRIDE), :] = rows
                if two_pass:
                    for r1 in range(DA_MAX_STRIDE):
                        xn_ref[pl.ds(r1, part, stride=DA_MAX_STRIDE), :] = tmp_ref[r1 * part:(r1 + 1) * part, :]

    @pl.when(grp == 0)
    def _():
        num_ref[...] = on_ref[...]
        den_ref[...] = jnp.ones_like(den_ref)
        mx_ref[...] = ln_ref[...]

    @pl.when(grp > 0)
    def _():
        m_old = mx_ref[...]
        lse = ln_ref[...]
        m_new = jnp.maximum(m_old, lse)
        a_old = jnp.exp(m_old - m_new)
        a_new = jnp.exp(lse - m_new)
        num_ref[...] = num_ref[...] * a_old + on_ref[...] * a_new
        den_ref[...] = den_ref[...] * a_old + a_new
        mx_ref[...] = m_new

    @pl.when(grp == DA_NGROUPS - 1)
    def _():
        o_ref[0] = (num_ref[...] / den_ref[...]).astype(o_ref.dtype)


def _dilated(daq, dak, dav):
    B, S, _ = daq.shape
    W = V7X_LANES
    pairs = DA_GROUP_WIDTH // W
    max_dil = max(d for _, d in DA_CONFIGS)
    grp_spec = pl.BlockSpec((1, S, W), lambda b, p, g: (b, 0, g * pairs + p))
    full = lambda dt: pltpu.VMEM((S, W), dt)
    return pl.pallas_call(
        functools.partial(_dilated_kernel, S=S),
        grid=(B, pairs, DA_NGROUPS),
        in_specs=[grp_spec, grp_spec, grp_spec],
        out_specs=pl.BlockSpec((1, S, W), lambda b, p, g: (b, 0, p)),
        out_shape=jax.ShapeDtypeStruct((B, S, DA_GROUP_WIDTH), BF16),
        scratch_shapes=[
            full(BF16),
            pltpu.VMEM((S + max_dil * DA_QBLOCK, W), BF16),
            pltpu.VMEM((S + max_dil * DA_QBLOCK, W), BF16),
            full(F32), full(F32), full(F32), full(F32), full(F32), full(F32), full(F32),
            full(F32),
        ],
        compiler_params=_params("parallel", "parallel", "arbitrary"),
        name="dilated",
    )(daq, dak, dav)


def _mix_out_kernel(x_ref, gain_ref, oa_ref, z_ref, onorm_ref, yb_ref, yc_ref, wg_ref, bg_ref, wa_ref, wb_ref,
                    wc_ref, wo_ref, o_ref):
    x = x_ref[...]
    D = x.shape[-1]
    h = _rms(x, gain_ref[...]).astype(BF16)
    o = oa_ref[...]
    z = z_ref[...]
    heads = []
    for hd in range(DN_HEADS):
        cols = slice(hd * DN_HEAD_DIM, (hd + 1) * DN_HEAD_DIM)
        zh = z[:, cols]
        heads.append((_rms(o[:, cols], onorm_ref[...]) * _silu(zh)).astype(BF16))
    ya = jnp.concatenate(heads, axis=1)
    merged = jnp.zeros(x.shape, F32)
    for i, (y, w_ref) in enumerate(((ya, wa_ref), (yb_ref[...], wb_ref), (yc_ref[...], wc_ref))):
        gate = _sigmoid(_dot(h, wg_ref[:, i * D:(i + 1) * D]) + bg_ref[:, i * D:(i + 1) * D])
        merged = merged + gate * _dot(y, w_ref[...])
    o_ref[...] = x + _dot(merged.astype(BF16), wo_ref[...])


def _mix_out(x, gain, oa, z, onorm, yb, yc, wg, bg, wa, wb, wc, wo):
    T, D = x.shape
    tm = MIX_TOKEN_TILE
    tok = lambda n: pl.BlockSpec((tm, n), lambda i: (i, 0))
    return pl.pallas_call(
        _mix_out_kernel,
        grid=(T // tm,),
        in_specs=[
            tok(D), _resident((1, D)),
            tok(oa.shape[1]), tok(z.shape[1]), _resident(onorm.shape), tok(yb.shape[1]), tok(yc.shape[1]),
            _resident(wg.shape), _resident(bg.shape), _resident(wa.shape), _resident(wb.shape),
            _resident(wc.shape), _resident(wo.shape),
        ],
        out_specs=tok(D),
        out_shape=jax.ShapeDtypeStruct((T, D), F32),
        compiler_params=_params("parallel"),
        name="mix_out",
    )(x, gain, oa, z, onorm, yb, yc, wg, bg, wa, wb, wc, wo)


def _pair_layout_columns():
    half = DA_HEAD_DIM // 2
    order = []
    for g in range(DA_NGROUPS):
        for p in range(DA_HEADS_PER_GROUP // 2):
            for part in range(2):
                for e in range(2):
                    head = 2 * p + e
                    start = g * DA_GROUP_WIDTH + head * DA_HEAD_DIM + part * half
                    order.extend(range(start, start + half))
    return jnp.asarray(order, dtype=jnp.int32)


def _rope_tables(seq_len):
    half = DA_HEAD_DIM // 2
    inv_freq = ROPE_THETA ** (-jnp.arange(half, dtype=F32) / half)
    ang = jnp.arange(seq_len).astype(F32)[:, None] * inv_freq[None, :]
    cos = jnp.tile(jnp.cos(ang), (1, V7X_LANES // half))
    sin = jnp.sin(ang)
    sin = jnp.concatenate([-sin, -sin, sin, sin], axis=1)
    return cos, sin


def kernel(x, ffn1_norm, ffn1_w_gate, ffn1_w_up, ffn1_w_down, mix_norm, w_in, dn_conv, dn_a_log, dn_dt_bias, dn_out_norm, pool_w, pool_scale, w_proj_a, w_proj_b, w_proj_c, w_gate, b_gate, w_out, ffn2_norm, ffn2_w_gate, ffn2_w_up, ffn2_w_down, final_norm):
    B, S, D = x.shape
    depth = w_in.shape[0]
    T = B * S
    off_z = 3 * DN_WIDTH
    off_beta = off_z + DN_WIDTH
    off_pool = off_beta + 4 * DN_HEADS
    off_da = off_pool + POOL_WIDTH
    pair_cols = _pair_layout_columns()
    cos, sin = _rope_tables(S)
    fgain = final_norm.reshape(1, D)

    xt = x.reshape(T, D)
    for l in range(depth):
        xt = _ffn(xt, ffn1_norm[l].reshape(1, D), ffn1_w_gate[l].astype(BF16), ffn1_w_up[l].astype(BF16),
                  ffn1_w_down[l].astype(BF16), fgain, final_norm=False)

        w = w_in[l]
        w_da = w[:, off_da:]
        w_da = jnp.concatenate([w_da[:, :DA_WIDTH][:, pair_cols],
                                w_da[:, DA_WIDTH:2 * DA_WIDTH][:, pair_cols],
                                w_da[:, 2 * DA_WIDTH:]], axis=1)
        w_ba = jnp.pad(w[:, off_beta:off_pool], ((0, 0), (0, V7X_LANES - 4 * DN_HEADS)))
        qkv, z, bg, u, daq, dak, dav = _mix_in(
            xt, mix_norm[l].reshape(1, D), w[:, :off_z].astype(BF16), w[:, off_z:off_beta].astype(BF16),
            w_ba.astype(BF16), w[:, off_pool:off_da].astype(BF16), w_da.astype(BF16), dn_conv[l],
            dn_a_log[l], dn_dt_bias[l], cos, sin, S)

        o_a = _deltanet(qkv.reshape(B, S, -1), bg.reshape(B, S, -1))
        y_b = _pool(u.reshape(B, S, -1), pool_w[l], pool_scale[l].reshape(1, POOL_WIDTH))
        y_c = _dilated(daq.reshape(B, S, -1), dak.reshape(B, S, -1), dav.reshape(B, S, -1))

        xt = _mix_out(xt, mix_norm[l].reshape(1, D), o_a.reshape(T, -1), z,
                      dn_out_norm[l].reshape(1, DN_HEAD_DIM), y_b.reshape(T, -1),
                      y_c.reshape(T, -1), w_gate[l].astype(BF16), b_gate[l].reshape(1, -1),
                      w_proj_a[l].astype(BF16), w_proj_b[l].astype(BF16), w_proj_c[l].astype(BF16),
                      w_out[l].astype(BF16))

        xt = _ffn(xt, ffn2_norm[l].reshape(1, D), ffn2_w_gate[l].astype(BF16), ffn2_w_up[l].astype(BF16),
                  ffn2_w_down[l].astype(BF16), fgain, final_norm=(l == depth - 1))
    return xt.reshape(B, S, D)
```
